```python
import math
import jax, jax.numpy as jnp
from jax import lax
import numpy as np

D_MODEL = 1024
BATCH = 8
SEQ = 4096
DEPTH = 2

RET_HEADS = 8
RET_QK = 64
RET_V = 64
RET_WIDTH = RET_HEADS * RET_V
DIFF_HEADS = 4
DIFF_QK = 64
DIFF_V = 2 * DIFF_QK
DIFF_WIDTH = DIFF_HEADS * DIFF_V
MIX_WIDTH = RET_WIDTH + DIFF_WIDTH
W_RQ = RET_HEADS * RET_QK
W_RK = RET_HEADS * RET_QK
W_RV = RET_WIDTH
W_RG = RET_WIDTH
W_DQ = DIFF_HEADS * 2 * DIFF_QK
W_DK = DIFF_HEADS * 2 * DIFF_QK
W_DV = DIFF_WIDTH
IN_WIDTH = W_RQ + W_RK + W_RV + W_RG + W_DQ + W_DK + W_DV
IN_SPLITS = (W_RQ, W_RQ + W_RK, W_RQ + W_RK + W_RV, W_RQ + W_RK + W_RV + W_RG,
             W_RQ + W_RK + W_RV + W_RG + W_DQ, W_RQ + W_RK + W_RV + W_RG + W_DQ + W_DK)
CHUNK = 128
Q_BLOCK = 128
ROPE_BASE = 10000.0
SUBLN_EPS = 1e-5
LN_EPS = 1e-5
N_EXPERTS = 16
N_GROUPS = 4
EXPERTS_PER_GROUP = N_EXPERTS // N_GROUPS
TOP_K = 2
D_EXPERT = 512
ALPHA = (2 * DEPTH) ** 0.25
BETA = (8 * DEPTH) ** -0.25
N_MOD = 6

kernel_name = 'hybrid_retention_diffattn_moe_deepnorm_adaln'


def _layer_norm(x, eps=LN_EPS):
    xf = x.astype(jnp.float32)
    mu = jnp.mean(xf, axis=-1, keepdims=True)
    var = jnp.mean(jnp.square(xf - mu), axis=-1, keepdims=True)
    return ((xf - mu) * lax.rsqrt(var + eps)).astype(x.dtype)


def _layer_norm_affine(x, g, b):
    return _layer_norm(x) * g + b


def _rms_norm(x, g, eps):
    xf = x.astype(jnp.float32)
    y = xf * lax.rsqrt(jnp.mean(jnp.square(xf), axis=-1, keepdims=True) + eps)
    return (y * g.astype(jnp.float32)).astype(x.dtype)


def _rotary(x, pos):
    d = x.shape[-1]
    inv = 1.0 / (ROPE_BASE ** (jnp.arange(0, d, 2, dtype=jnp.float32) / d))
    ang = pos.astype(jnp.float32)[:, None] * inv[None, :]
    cos = jnp.cos(ang)[None, :, None, :]
    sin = jnp.sin(ang)[None, :, None, :]
    xf = x.astype(jnp.float32)
    x1, x2 = xf[..., : d // 2], xf[..., d // 2:]
    return jnp.concatenate([x1 * cos - x2 * sin, x1 * sin + x2 * cos], axis=-1).astype(x.dtype)


def _retention(q, k, v):
    B, S, H, dk = q.shape
    dv = v.shape[-1]
    n = S // CHUNK
    log_g = jnp.log(1.0 - 2.0 ** (-5.0 - jnp.arange(H, dtype=jnp.float32)))
    idx = jnp.arange(CHUNK, dtype=jnp.float32)
    rel = idx[:, None] - idx[None, :]
    decay = jnp.where(rel >= 0, jnp.exp(log_g[:, None, None] * jnp.maximum(rel, 0.0)), 0.0)
    qc = q.reshape(B, n, CHUNK, H, dk) * (dk ** -0.5)
    kc = k.reshape(B, n, CHUNK, H, dk)
    vc = v.reshape(B, n, CHUNK, H, dv)
    scores = jnp.einsum('bnihd,bnjhd->bnhij', qc, kc) * decay[None, None]
    inner = jnp.einsum('bnhij,bnjhe->bnihe', scores, vc)
    k_decay = jnp.exp(log_g[None, :] * (CHUNK - 1 - idx)[:, None])
    chunk_kv = jnp.einsum('bnjhd,jh,bnjhe->nbhde', kc, k_decay, vc).astype(jnp.float32)
    g_chunk = jnp.exp(log_g * CHUNK)[None, :, None, None]

    def step(state, kv):
        return state * g_chunk + kv, state

    _, prev = lax.scan(step, jnp.zeros_like(chunk_kv[0]), chunk_kv)
    q_decay = jnp.exp(log_g[None, :] * (idx + 1.0)[:, None])
    cross = jnp.einsum('bnihd,nbhde,ih->bnihe', qc, prev, q_decay)
    return (inner + cross).reshape(B, S, H, dv).astype(v.dtype)


def _diff_attention(q, k, v, lam, subln_g, lambda_init):
    B, S, H, _, d = q.shape
    nb = S // Q_BLOCK
    qb = (q * (d ** -0.5)).reshape(B, nb, Q_BLOCK, H, 2, d).transpose(1, 0, 2, 3, 4, 5)
    kpos = jnp.arange(S)
    starts = jnp.arange(nb) * Q_BLOCK

    def block(args):
        qi, start = args
        s = jnp.einsum('bqhcd,bkhcd->bhcqk', qi, k).astype(jnp.float32)
        qpos = start + jnp.arange(Q_BLOCK)
        s = jnp.where((kpos[None, :] <= qpos[:, None])[None, None, None], s, -jnp.inf)
        p = jax.nn.softmax(s, axis=-1)
        a = p[:, :, 0] - lam * p[:, :, 1]
        return jnp.einsum('bhqk,bkhe->bqhe', a.astype(v.dtype), v)

    out = lax.map(block, (qb, starts))
    out = out.transpose(1, 0, 2, 3, 4).reshape(B, S, H, 2 * d)
    return _rms_norm(out, subln_g, SUBLN_EPS) * (1.0 - lambda_init)


def _moe(h, router_w, router_b, w_gate, w_up, w_down):
    B, S, D = h.shape
    t = h.reshape(B * S, D)
    logits = jnp.matmul(t, router_w).astype(jnp.float32) + router_b.astype(jnp.float32)
    probs = jax.nn.softmax(logits, axis=-1)
    grouped = probs.reshape(-1, N_GROUPS, EXPERTS_PER_GROUP)
    top_in_group, _ = lax.top_k(grouped, TOP_K)
    _, g_idx = lax.top_k(jnp.sum(top_in_group, axis=-1), 1)
    sel = jnp.einsum('tg,tge->te', jax.nn.one_hot(g_idx[:, 0], N_GROUPS, dtype=jnp.float32), grouped)
    top_p, top_i = lax.top_k(sel, TOP_K)
    gates = top_p / jnp.sum(top_p, axis=-1, keepdims=True)
    expert_idx = g_idx * EXPERTS_PER_GROUP + top_i
    combine = jnp.einsum('tk,tke->te', gates, jax.nn.one_hot(expert_idx, N_EXPERTS, dtype=jnp.float32))
    combine = combine.astype(t.dtype)
    out = jnp.zeros_like(t)
    for e in range(N_EXPERTS):
        he = jax.nn.silu(jnp.matmul(t, w_gate[e])) * jnp.matmul(t, w_up[e])
        out = out + combine[:, e:e + 1] * jnp.matmul(he, w_down[e])
    return out.reshape(B, S, D)


def setup_inputs(seed: int = 0) -> dict:
    key = jax.random.key(seed)
    ks = jax.random.split(key, 20)
    f32 = jnp.float32
    D = D_MODEL
    x = jax.random.normal(ks[0], (BATCH, SEQ, D), f32)
    c = jax.random.normal(ks[1], (BATCH, D), f32)
    w_ada = jax.random.normal(ks[2], (DEPTH, D, N_MOD * D), f32) * (0.2 * D ** -0.5)
    gate_offset = jnp.repeat(jnp.array([0.0, 0.0, 1.0, 0.0, 0.0, 1.0], f32), D)
    b_ada = 0.01 * jax.random.normal(ks[3], (DEPTH, N_MOD * D), f32) + gate_offset[None, :]
    col_scale = jnp.concatenate([
        jnp.ones((W_RQ + W_RK,), f32), jnp.full((W_RV,), BETA, f32), jnp.ones((W_RG + W_DQ + W_DK,), f32),
        jnp.full((W_DV,), BETA, f32)])
    w_in = jax.random.normal(ks[4], (DEPTH, D, IN_WIDTH), f32) * (D ** -0.5) * col_scale[None, None, :]
    w_out = jax.random.normal(ks[5], (DEPTH, MIX_WIDTH, D), f32) * (MIX_WIDTH ** -0.5) * BETA
    lambda_q1 = 0.1 * jax.random.normal(ks[6], (DEPTH, DIFF_QK), f32)
    lambda_k1 = 0.1 * jax.random.normal(ks[7], (DEPTH, DIFF_QK), f32)
    lambda_q2 = 0.1 * jax.random.normal(ks[8], (DEPTH, DIFF_QK), f32)
    lambda_k2 = 0.1 * jax.random.normal(ks[9], (DEPTH, DIFF_QK), f32)
    diff_subln = 1.0 + 0.02 * jax.random.normal(ks[10], (DEPTH, DIFF_V), f32)
    ln_mix_g = 1.0 + 0.02 * jax.random.normal(ks[11], (DEPTH, D), f32)
    ln_mix_b = 0.02 * jax.random.normal(ks[12], (DEPTH, D), f32)
    ln_ffn_g = 1.0 + 0.02 * jax.random.normal(ks[13], (DEPTH, D), f32)
    ln_ffn_b = 0.02 * jax.random.normal(ks[14], (DEPTH, D), f32)
    router_w = jax.random.normal(ks[15], (D, N_EXPERTS), f32) * (D ** -0.5)
    router_b = 0.01 * jax.random.normal(ks[16], (N_EXPERTS,), f32)
    w_gate = jax.random.normal(ks[17], (DEPTH, N_EXPERTS, D, D_EXPERT), f32) * (D ** -0.5) * BETA
    w_up = jax.random.normal(ks[18], (DEPTH, N_EXPERTS, D, D_EXPERT), f32) * (D ** -0.5) * BETA
    w_down = jax.random.normal(ks[19], (DEPTH, N_EXPERTS, D_EXPERT, D), f32) * (D_EXPERT ** -0.5) * BETA
    return {'x': x, 'c': c, 'w_ada': w_ada, 'b_ada': b_ada, 'w_in': w_in, 'w_out': w_out,
            'lambda_q1': lambda_q1, 'lambda_k1': lambda_k1, 'lambda_q2': lambda_q2, 'lambda_k2': lambda_k2,
            'diff_subln': diff_subln, 'ln_mix_g': ln_mix_g, 'ln_mix_b': ln_mix_b,
            'ln_ffn_g': ln_ffn_g, 'ln_ffn_b': ln_ffn_b, 'router_w': router_w, 'router_b': router_b,
            'w_gate': w_gate, 'w_up': w_up, 'w_down': w_down}


def reference(x, c, w_ada, b_ada, w_in, w_out, lambda_q1, lambda_k1, lambda_q2, lambda_k2,
              diff_subln, ln_mix_g, ln_mix_b, ln_ffn_g, ln_ffn_b, router_w, router_b,
              w_gate, w_up, w_down):
    B, S, D = x.shape
    pos = jnp.arange(S)
    cond = jax.nn.silu(c)
    for l in range(DEPTH):
        mod = jnp.matmul(cond, w_ada[l]) + b_ada[l]
        sh1, sc1, g1, sh2, sc2, g2 = jnp.split(mod, N_MOD, axis=-1)
        h = _layer_norm(x) * (1.0 + sc1[:, None, :]) + sh1[:, None, :]
        proj = jnp.matmul(h, w_in[l])
        rq, rk, rv, rg, dq, dk, dv = jnp.split(proj, IN_SPLITS, axis=-1)
        rq = _rotary(rq.reshape(B, S, RET_HEADS, RET_QK), pos)
        rk = _rotary(rk.reshape(B, S, RET_HEADS, RET_QK), pos)
        ret = _retention(rq, rk, rv.reshape(B, S, RET_HEADS, RET_V))
        ret = _layer_norm(ret).reshape(B, S, RET_WIDTH) * jax.nn.silu(rg)
        lambda_init = 0.8 - 0.6 * math.exp(-0.3 * l)
        lam = (jnp.exp(jnp.sum(lambda_q1[l].astype(jnp.float32) * lambda_k1[l].astype(jnp.float32)))
               - jnp.exp(jnp.sum(lambda_q2[l].astype(jnp.float32) * lambda_k2[l].astype(jnp.float32)))
               + lambda_init)
        dif = _diff_attention(dq.reshape(B, S, DIFF_HEADS, 2, DIFF_QK),
                              dk.reshape(B, S, DIFF_HEADS, 2, DIFF_QK),
                              dv.reshape(B, S, DIFF_HEADS, DIFF_V),
                              lam, diff_subln[l], lambda_init).reshape(B, S, DIFF_WIDTH)
        mixed = jnp.matmul(jnp.concatenate([ret, dif], axis=-1), w_out[l])
        x = _layer_norm_affine(ALPHA * x + g1[:, None, :] * mixed, ln_mix_g[l], ln_mix_b[l])
        h = _layer_norm(x) * (1.0 + sc2[:, None, :]) + sh2[:, None, :]
        y = _moe(h, router_w, router_b, w_gate[l], w_up[l], w_down[l])
        x = _layer_norm_affine(ALPHA * x + g2[:, None, :] * y, ln_ffn_g[l], ln_ffn_b[l])
    return x
```

```python
import functools
import math

import jax
import jax.numpy as jnp
from jax import lax
from jax.experimental import pallas as pl
from jax.experimental.pallas import tpu as pltpu

F32 = jnp.float32
BF16 = jnp.bfloat16
U32 = jnp.uint32
I32 = jnp.int32

RET_HEADS = 8
RET_QK = 64
DIFF_HEADS = 4
DIFF_QK = 64
HEAD_PAIR = 128
SEG = 512
N_SEG = 7
CHUNK = 128
ROPE_BASE = 10000.0
SUBLN_EPS = 1e-5
LN_EPS = 1e-5
N_EXPERTS = 16
N_GROUPS = 4
EXPERTS_PER_GROUP = 4
TOP_K = 2
N_MOD = 6
LANES = 128
VMEM_LIMIT = 56 * 1024 * 1024

TM_PROJ = 512
BQ = 256
TS_RANK = 512
TM_DISP = 256
TM_EXP = 256


def _params(*sem):
    return pltpu.CompilerParams(dimension_semantics=sem, vmem_limit_bytes=VMEM_LIMIT)


def _layer_norm(x):
    mu = jnp.mean(x, axis=-1, keepdims=True)
    xc = x - mu
    var = jnp.mean(xc * xc, axis=-1, keepdims=True)
    return xc * lax.rsqrt(var + LN_EPS)


def _silu(x):
    return x * jax.nn.sigmoid(x)


def _adaln_kernel(c_ref, w_ref, b_ref, o_ref):
    cond = _silu(c_ref[...]).astype(BF16)
    o_ref[0] = jnp.dot(cond, w_ref[0].astype(BF16), preferred_element_type=F32) + b_ref[0]


def _adaln(c, w_ada, b_ada):
    depth, d, n = w_ada.shape
    b = c.shape[0]
    tn = 1536
    return pl.pallas_call(
        _adaln_kernel,
        grid=(depth, n // tn),
        in_specs=[pl.BlockSpec((b, d), lambda l, j: (0, 0)),
                  pl.BlockSpec((1, d, tn), lambda l, j: (l, 0, j)),
                  pl.BlockSpec((1, 1, tn), lambda l, j: (l, 0, j))],
        out_specs=pl.BlockSpec((1, b, tn), lambda l, j: (l, 0, j)),
        out_shape=jax.ShapeDtypeStruct((depth, b, n), F32),
        compiler_params=_params("arbitrary", "arbitrary"),
        name="adaln",
    )(c, w_ada, b_ada.reshape(depth, 1, n))


def _inproj_kernel(x_ref, sc_ref, sh_ref, w_ref, cos_ref, sin_ref, *out_refs):
    h = (_layer_norm(x_ref[...]) * (1.0 + sc_ref[0]) + sh_ref[0]).astype(BF16)
    cos = cos_ref[...]
    sin = sin_ref[...]
    lane = lax.broadcasted_iota(I32, cos.shape, 1)
    first_half = (lane & (RET_QK - 1)) < (RET_QK // 2)

    def rotary(p):
        outs = []
        for c in range(SEG // LANES):
            pc = p[:, c * LANES:(c + 1) * LANES]
            swapped = jnp.where(first_half, pltpu.roll(pc, LANES - RET_QK // 2, 1),
                                pltpu.roll(pc, RET_QK // 2, 1))
            outs.append(pc * cos + swapped * sin)
        return jnp.concatenate(outs, axis=1)

    for s, o_ref in enumerate(out_refs):
        p = jnp.dot(h, w_ref[:, s * SEG:(s + 1) * SEG], preferred_element_type=F32)
        if s in (0, 1):
            p = rotary(p)
        if s in (0, 4):
            p = p * (RET_QK ** -0.5)
        o_ref[...] = p.astype(BF16)


def _inproj(x2d, sc, sh, w_bf, cos_t, sin_t, seq):
    t, d = x2d.shape
    tm = TM_PROJ
    per_b = seq // tm
    return pl.pallas_call(
        _inproj_kernel,
        grid=(t // tm,),
        in_specs=[pl.BlockSpec((tm, d), lambda i: (i, 0)),
                  pl.BlockSpec((1, 1, d), lambda i: (i // per_b, 0, 0)),
                  pl.BlockSpec((1, 1, d), lambda i: (i // per_b, 0, 0)),
                  pl.BlockSpec((d, N_SEG * SEG), lambda i: (0, 0)),
                  pl.BlockSpec((tm, LANES), lambda i: (i % per_b, 0)),
                  pl.BlockSpec((tm, LANES), lambda i: (i % per_b, 0))],
        out_specs=[pl.BlockSpec((tm, SEG), lambda i: (i, 0))] * N_SEG,
        out_shape=[jax.ShapeDtypeStruct((t, SEG), BF16)] * N_SEG,
        compiler_params=_params("arbitrary"),
        name="inproj",
    )(x2d, sc, sh, w_bf, cos_t, sin_t)


def _ret_kernel(q_ref, k_ref, v_ref, g_ref, d2_ref, qd_ref, kd_ref, gm_ref, o_ref, *, n_chunks):
    c = CHUNK
    lo = lax.broadcasted_iota(I32, (c, HEAD_PAIR), 1) < RET_QK
    r = lax.broadcasted_iota(I32, (HEAD_PAIR, HEAD_PAIR), 0) < RET_QK
    cc = lax.broadcasted_iota(I32, (HEAD_PAIR, HEAD_PAIR), 1) < RET_QK
    same_head = r == cc
    d2 = d2_ref[0]
    qd = qd_ref[0]
    kd = kd_ref[0]
    gm = gm_ref[0]

    def split_heads(a):
        zero = jnp.zeros_like(a)
        return jnp.concatenate([jnp.where(lo, a, zero), jnp.where(lo, zero, a)], axis=0)

    def body(n, state):
        sl = pl.ds(pl.multiple_of(n * c, c), c)
        q = q_ref[sl, :]
        k = k_ref[sl, :]
        v = v_ref[sl, :]
        s2 = lax.dot_general(split_heads(q), k, (((1,), (1,)), ((), ())), preferred_element_type=F32)
        p2 = (s2 * d2).astype(BF16)
        pcat = jnp.concatenate([p2[:c], p2[c:]], axis=1)
        inner = jnp.dot(pcat, split_heads(v), preferred_element_type=F32)
        qdq = (q.astype(F32) * qd).astype(BF16)
        cross = jnp.dot(qdq, state.astype(BF16), preferred_element_type=F32)
        o = inner + cross
        kdk = (k.astype(F32) * kd).astype(BF16)
        kv = lax.dot_general(kdk, v, (((0,), (0,)), ((), ())), preferred_element_type=F32)
        state = state * gm + jnp.where(same_head, kv, 0.0)
        inv = 1.0 / RET_QK
        s_all = jnp.sum(o, axis=-1, keepdims=True)
        s_lo = jnp.sum(jnp.where(lo, o, 0.0), axis=-1, keepdims=True)
        dlt = o - jnp.where(lo, s_lo, s_all - s_lo) * inv
        dd = dlt * dlt
        v_all = jnp.sum(dd, axis=-1, keepdims=True)
        v_lo = jnp.sum(jnp.where(lo, dd, 0.0), axis=-1, keepdims=True)
        var = jnp.where(lo, v_lo, v_all - v_lo) * inv
        y = dlt * lax.rsqrt(var + LN_EPS)
        o_ref[sl, :] = (y * _silu(g_ref[sl, :].astype(F32))).astype(BF16)
        return state

    lax.fori_loop(0, n_chunks, body, jnp.zeros((HEAD_PAIR, HEAD_PAIR), F32))


def _retention_tables():
    c = CHUNK
    log_g = jnp.log(1.0 - 2.0 ** (-5.0 - jnp.arange(RET_HEADS, dtype=F32)))
    idx = jnp.arange(c, dtype=F32)
    rel = idx[:, None] - idx[None, :]
    decay = jnp.where(rel >= 0, jnp.exp(log_g[:, None, None] * jnp.maximum(rel, 0.0)), 0.0)
    d2 = decay.reshape(RET_HEADS // 2, 2 * c, c)
    lane_head = jnp.arange(HEAD_PAIR) // RET_QK
    pair_log = log_g.reshape(RET_HEADS // 2, 2)[:, lane_head]
    qd = jnp.exp(pair_log[:, None, :] * (idx + 1.0)[None, :, None])
    kd = jnp.exp(pair_log[:, None, :] * (c - 1.0 - idx)[None, :, None])
    gm = jnp.broadcast_to(jnp.exp(pair_log * c)[:, :, None], (RET_HEADS // 2, HEAD_PAIR, HEAD_PAIR))
    return d2, qd, kd, gm


def _retention(rq, rk, rv, rg, batch, seq):
    t = rq.shape[0]
    n_pairs = RET_HEADS // 2
    d2, qd, kd, gm = _retention_tables()
    seq_spec = pl.BlockSpec((seq, HEAD_PAIR), lambda b, p: (b, p))
    return pl.pallas_call(
        functools.partial(_ret_kernel, n_chunks=seq // CHUNK),
        grid=(batch, n_pairs),
        in_specs=[seq_spec, seq_spec, seq_spec, seq_spec,
                  pl.BlockSpec((1, 2 * CHUNK, CHUNK), lambda b, p: (p, 0, 0)),
                  pl.BlockSpec((1, CHUNK, HEAD_PAIR), lambda b, p: (p, 0, 0)),
                  pl.BlockSpec((1, CHUNK, HEAD_PAIR), lambda b, p: (p, 0, 0)),
                  pl.BlockSpec((1, HEAD_PAIR, HEAD_PAIR), lambda b, p: (p, 0, 0))],
        out_specs=seq_spec,
        out_shape=jax.ShapeDtypeStruct((t, SEG), BF16),
        compiler_params=_params("arbitrary", "arbitrary"),
        name="retention",
    )(rq, rk, rv, rg, d2, qd, kd, gm)


def _diff_kernel(lq1_ref, lk1_ref, lq2_ref, lk2_ref, sg_ref, q_ref, k_ref, v_ref, o_ref, *, lambda_init):
    bq = q_ref.shape[0]
    qi = pl.program_id(2)
    lam = (jnp.exp(jnp.sum(lq1_ref[...] * lk1_ref[...], axis=-1, keepdims=True))
           - jnp.exp(jnp.sum(lq2_ref[...] * lk2_ref[...], axis=-1, keepdims=True)) + lambda_init)
    q = q_ref[...]
    lo = lax.broadcasted_iota(I32, q.shape, 1) < DIFF_QK
    zero = jnp.zeros_like(q)
    q2 = jnp.concatenate([jnp.where(lo, q, zero), jnp.where(lo, zero, q)], axis=0)

    def step(j, carry, masked):
        m, l, acc = carry
        sl = pl.ds(pl.multiple_of(j * bq, bq), bq)
        s = lax.dot_general(q2, k_ref[sl, :], (((1,), (1,)), ((), ())), preferred_element_type=F32)
        if masked:
            row = lax.broadcasted_iota(I32, s.shape, 0)
            col = lax.broadcasted_iota(I32, s.shape, 1)
            qrow = jnp.where(row >= bq, row - bq, row)
            s = jnp.where(col <= qrow, s, -jnp.inf)
        m_new = jnp.maximum(m, jnp.max(s, axis=-1, keepdims=True))
        alpha = jnp.exp(m - m_new)
        p = jnp.exp(s - m_new)
        l = alpha * l + jnp.sum(p, axis=-1, keepdims=True)
        acc = alpha * acc + jnp.dot(p.astype(BF16), v_ref[sl, :], preferred_element_type=F32)
        return m_new, l, acc

    init = (jnp.full((2 * bq, 1), -jnp.inf, F32), jnp.zeros((2 * bq, 1), F32),
            jnp.zeros((2 * bq, HEAD_PAIR), F32))
    carry = lax.fori_loop(0, qi, lambda j, cr: step(j, cr, False), init)
    _, l, acc = step(qi, carry, True)
    a = acc / l
    out = a[:bq] - lam * a[bq:]
    ms = jnp.mean(out * out, axis=-1, keepdims=True)
    out = out * lax.rsqrt(ms + SUBLN_EPS) * sg_ref[...] * (1.0 - lambda_init)
    o_ref[...] = out.astype(BF16)


def _diff_attention(dq, dk, dv, lq1, lk1, lq2, lk2, subln, lambda_init, batch, seq):
    t = dq.shape[0]
    nq = seq // BQ
    lam_spec = pl.BlockSpec((1, DIFF_QK), lambda b, h, i: (0, 0))
    kv_spec = pl.BlockSpec((seq, HEAD_PAIR), lambda b, h, i: (b, h))
    q_spec = pl.BlockSpec((BQ, HEAD_PAIR), lambda b, h, i: (b * nq + i, h))
    return pl.pallas_call(
        functools.partial(_diff_kernel, lambda_init=lambda_init),
        grid=(batch, DIFF_HEADS, nq),
        in_specs=[lam_spec, lam_spec, lam_spec, lam_spec,
                  pl.BlockSpec((1, HEAD_PAIR), lambda b, h, i: (0, 0)),
                  q_spec, kv_spec, kv_spec],
        out_specs=q_spec,
        out_shape=jax.ShapeDtypeStruct((t, SEG), BF16),
        compiler_params=_params("arbitrary", "arbitrary", "arbitrary"),
        name="diff_attention",
    )(lq1.reshape(1, -1), lk1.reshape(1, -1), lq2.reshape(1, -1), lk2.reshape(1, -1),
      subln.reshape(1, -1), dq, dk, dv)


def _pack_bf16_pair(a, b):
    ua = lax.bitcast_convert_type(a.astype(BF16).astype(F32), U32)
    ub = lax.bitcast_convert_type(b.astype(BF16).astype(F32), U32)
    return (ua >> 16) | (ub & jnp.uint32(0xFFFF0000))


def _unpack_bf16_pair(w):
    a = lax.bitcast_convert_type(w << 16, F32)
    b = lax.bitcast_convert_type(w & jnp.uint32(0xFFFF0000), F32)
    return a, b


PACK_SUB = 4


def _store_packed_rows(ref, lo_half, hi_half):
    packed = _pack_bf16_pair(lo_half, hi_half)
    for c in range(PACK_SUB):
        ref[:, c, :] = packed[:, c * LANES:(c + 1) * LANES]


def _load_packed_rows(ref):
    parts = [_unpack_bf16_pair(ref[:, c, :]) for c in range(PACK_SUB)]
    return jnp.concatenate([p[0] for p in parts] + [p[1] for p in parts], axis=1)


def _first_top2(vals):
    m1 = jnp.maximum(jnp.maximum(vals[0], vals[1]), jnp.maximum(vals[2], vals[3]))
    i1 = jnp.where(vals[0] == m1, 0, jnp.where(vals[1] == m1, 1, jnp.where(vals[2] == m1, 2, 3)))
    rest = [jnp.where(i1 == i, -1.0, v) for i, v in enumerate(vals)]
    m2 = jnp.maximum(jnp.maximum(rest[0], rest[1]), jnp.maximum(rest[2], rest[3]))
    i2 = jnp.where(rest[0] == m2, 0, jnp.where(rest[1] == m2, 1, jnp.where(rest[2] == m2, 2, 3)))
    return m1, i1, m2, i2


def _outproj_kernel(ret_ref, dif_ref, x_ref, g1_ref, sc2_ref, sh2_ref, wo_ref, lng_ref, lnb_ref,
                    rw_ref, rb_ref, x1_ref, hp_ref, eidx_ref, gate_ref, *, alpha):
    half = ret_ref.shape[1]
    mixed = (jnp.dot(ret_ref[...], wo_ref[:half, :], preferred_element_type=F32)
             + jnp.dot(dif_ref[...], wo_ref[half:, :], preferred_element_type=F32))
    x1 = _layer_norm(alpha * x_ref[...] + g1_ref[0] * mixed) * lng_ref[...] + lnb_ref[...]
    x1_ref[...] = x1
    h2 = _layer_norm(x1) * (1.0 + sc2_ref[0]) + sh2_ref[0]
    hw = h2.shape[1] // 2
    _store_packed_rows(hp_ref, h2[:, :hw], h2[:, hw:])
    logits = lax.dot_general(rw_ref[...], h2.astype(BF16), (((1,), (1,)), ((), ())),
                             preferred_element_type=F32) + rb_ref[...]
    e = jnp.exp(logits - jnp.max(logits, axis=0, keepdims=True))
    probs = e / jnp.sum(e, axis=0, keepdims=True)
    rows = [probs[i:i + 1, :] for i in range(N_EXPERTS)]
    group_tops = []
    for g in range(N_GROUPS):
        m1, _, m2, _ = _first_top2(rows[g * EXPERTS_PER_GROUP:(g + 1) * EXPERTS_PER_GROUP])
        group_tops.append(m1 + m2)
    gmax = jnp.maximum(jnp.maximum(group_tops[0], group_tops[1]), jnp.maximum(group_tops[2], group_tops[3]))
    gbest = jnp.where(group_tops[0] == gmax, 0,
                      jnp.where(group_tops[1] == gmax, 1, jnp.where(group_tops[2] == gmax, 2, 3)))
    sel = []
    for i in range(EXPERTS_PER_GROUP):
        v = rows[3 * EXPERTS_PER_GROUP + i]
        for g in (2, 1, 0):
            v = jnp.where(gbest == g, rows[g * EXPERTS_PER_GROUP + i], v)
        sel.append(v)
    p0, i0, p1, i1 = _first_top2(sel)
    denom = p0 + p1
    eidx_ref[0:1, :] = gbest * EXPERTS_PER_GROUP + i0
    eidx_ref[1:2, :] = gbest * EXPERTS_PER_GROUP + i1
    tm = x1.shape[0]
    rid = lax.broadcasted_iota(I32, (LANES, tm), 0)
    gt = jnp.where(rid == 0, p0 / denom, jnp.where(rid == 1, p1 / denom, 0.0))
    gate_ref[...] = gt.T


def _outproj(ret, dif, x2d, g1, sc2, sh2, wo_bf, ln_g, ln_b, rw_t, rb, seq, alpha):
    t, d = x2d.shape
    tm = TM_PROJ
    per_b = seq // tm
    mod_spec = pl.BlockSpec((1, 1, d), lambda i: (i // per_b, 0, 0))
    row_spec = pl.BlockSpec((1, d), lambda i: (0, 0))
    return pl.pallas_call(
        functools.partial(_outproj_kernel, alpha=alpha),
        grid=(t // tm,),
        in_specs=[pl.BlockSpec((tm, SEG), lambda i: (i, 0)),
                  pl.BlockSpec((tm, SEG), lambda i: (i, 0)),
                  pl.BlockSpec((tm, d), lambda i: (i, 0)),
                  mod_spec, mod_spec, mod_spec,
                  pl.BlockSpec((2 * SEG, d), lambda i: (0, 0)),
                  row_spec, row_spec,
                  pl.BlockSpec((N_EXPERTS, d), lambda i: (0, 0)),
                  pl.BlockSpec((N_EXPERTS, 1), lambda i: (0, 0))],
        out_specs=[pl.BlockSpec((tm, d), lambda i: (i, 0)),
                   pl.BlockSpec((tm, PACK_SUB, LANES), lambda i: (i, 0, 0)),
                   pl.BlockSpec((TOP_K, tm), lambda i: (0, i)),
                   pl.BlockSpec((tm, LANES), lambda i: (i, 0))],
        out_shape=[jax.ShapeDtypeStruct((t, d), F32),
                   jax.ShapeDtypeStruct((t, PACK_SUB, LANES), U32),
                   jax.ShapeDtypeStruct((TOP_K, t), I32),
                   jax.ShapeDtypeStruct((t, LANES), F32)],
        compiler_params=_params("arbitrary"),
        name="outproj_router",
    )(ret, dif, x2d, g1, sc2, sh2, wo_bf, ln_g.reshape(1, d), ln_b.reshape(1, d), rw_t, rb.reshape(-1, 1))


def _rank_kernel(e_ref, tri_ref, rank_ref, cnt_ref, carry_ref):
    i = pl.program_id(0)

    @pl.when(i == 0)
    def _():
        carry_ref[...] = jnp.zeros_like(carry_ref)

    e = e_ref[...]
    ts = e.shape[1]
    onehot = lax.broadcasted_iota(I32, (N_EXPERTS, ts), 0) == e
    cum = jnp.dot(jnp.where(onehot, 1.0, 0.0).astype(BF16), tri_ref[...],
                  preferred_element_type=F32)
    carry = carry_ref[...]
    total = cum + carry[:, 0:1]
    rank_ref[...] = (jnp.sum(jnp.where(onehot, total, 0.0), axis=0, keepdims=True) - 1.0).astype(I32)
    new_carry = carry + cum[:, ts - 1:ts]
    carry_ref[...] = new_carry
    cnt_ref[...] = new_carry.astype(I32)


def _ranks(eidx):
    n_slots = eidx.size
    ts = TS_RANK
    tri = (jnp.arange(ts)[:, None] <= jnp.arange(ts)[None, :]).astype(BF16)
    return pl.pallas_call(
        _rank_kernel,
        grid=(n_slots // ts,),
        in_specs=[pl.BlockSpec((1, ts), lambda i: (0, i)),
                  pl.BlockSpec((ts, ts), lambda i: (0, 0))],
        out_specs=[pl.BlockSpec((1, ts), lambda i: (0, i)),
                   pl.BlockSpec((N_EXPERTS, LANES), lambda i: (0, 0))],
        out_shape=[jax.ShapeDtypeStruct((1, n_slots), I32),
                   jax.ShapeDtypeStruct((N_EXPERTS, LANES), I32)],
        scratch_shapes=[pltpu.VMEM((N_EXPERTS, LANES), F32)],
        compiler_params=_params("arbitrary"),
        name="slot_ranks",
    )(eidx.reshape(1, n_slots), tri)


def _row_copy(src_ref, r, dst_ref, p, sem):
    return pltpu.make_async_copy(src_ref.at[r], dst_ref.at[p], sem)


def _dispatch_kernel(off_ref, end_ref, cnt_ref, e_ref, rank_ref, hp_ref, xs_ref, zero_ref, sem, zsem):
    i = pl.program_id(0)
    tm = hp_ref.shape[0]
    tz = zero_ref.shape[0]

    @pl.when(i == 0)
    def _():
        zero_ref[...] = jnp.zeros_like(zero_ref)
        n_rows = xs_ref.shape[0]
        used = end_ref[N_EXPERTS - 1]

        def clear(start):
            return pltpu.make_async_copy(zero_ref, xs_ref.at[pl.ds(start, tz)], zsem)

        for wait in (False, True):
            for e in range(N_EXPERTS):
                @pl.when(cnt_ref[e] > 0)
                def _():
                    cp = clear(end_ref[e] - tz)
                    cp.wait() if wait else cp.start()

                @pl.when(used + e * tz < n_rows)
                def _():
                    cp = clear(used + e * tz)
                    cp.wait() if wait else cp.start()

    def issue(r, _):
        for k in range(TOP_K):
            p = off_ref[e_ref[0, k, r]] + rank_ref[0, k, r]
            _row_copy(hp_ref, r, xs_ref, p, sem).start()
        return 0

    lax.fori_loop(0, tm, issue, 0, unroll=8)

    def drain(r, _):
        for k in range(TOP_K):
            _row_copy(hp_ref, 0, xs_ref, 0, sem).wait()
        return 0

    lax.fori_loop(0, tm, drain, 0, unroll=8)


def _dispatch(off, end, cnt, e3, rank3, hp, n_rows):
    t = hp.shape[0]
    tm = TM_DISP
    smem_spec = pl.BlockSpec((1, TOP_K, tm), lambda i, *_: (i, 0, 0), memory_space=pltpu.SMEM)
    return pl.pallas_call(
        _dispatch_kernel,
        grid_spec=pltpu.PrefetchScalarGridSpec(
            num_scalar_prefetch=3,
            grid=(t // tm,),
            in_specs=[smem_spec, smem_spec,
                      pl.BlockSpec((tm, PACK_SUB, LANES), lambda i, *_: (i, 0, 0))],
            out_specs=pl.BlockSpec(memory_space=pl.ANY),
            scratch_shapes=[pltpu.VMEM((TM_EXP, PACK_SUB, LANES), U32), pltpu.SemaphoreType.DMA(()),
                            pltpu.SemaphoreType.DMA(())]),
        out_shape=jax.ShapeDtypeStruct((n_rows, PACK_SUB, LANES), U32),
        compiler_params=_params("arbitrary"),
        name="dispatch",
    )(off, end, cnt, e3, rank3, hp)


def _expert_kernel(te_ref, nv_ref, xs_ref, wg_ref, wu_ref, wd_ref, ys_ref):
    i = pl.program_id(0)

    @pl.when(i < nv_ref[0])
    def _():
        x = _load_packed_rows(xs_ref).astype(BF16)
        g = jnp.dot(x, wg_ref[0], preferred_element_type=F32)
        u = jnp.dot(x, wu_ref[0], preferred_element_type=F32)
        he = (_silu(g) * u).astype(BF16)
        y = jnp.dot(he, wd_ref[0], preferred_element_type=F32)
        hw = y.shape[1] // 2
        _store_packed_rows(ys_ref, y[:, :hw], y[:, hw:])

    @pl.when(i >= nv_ref[0])
    def _():
        ys_ref[...] = jnp.zeros_like(ys_ref)


def _experts(tile_expert, n_valid, xs, wg_bf, wu_bf, wd_bf):
    n_rows = xs.shape[0]
    tm = TM_EXP
    _, d, de = wg_bf.shape

    def row_map(i, te, nv):
        return (jnp.minimum(i, nv[0] - 1), 0, 0)

    return pl.pallas_call(
        _expert_kernel,
        grid_spec=pltpu.PrefetchScalarGridSpec(
            num_scalar_prefetch=2,
            grid=(n_rows // tm,),
            in_specs=[pl.BlockSpec((tm, PACK_SUB, LANES), row_map),
                      pl.BlockSpec((1, d, de), lambda i, te, nv: (te[i], 0, 0)),
                      pl.BlockSpec((1, d, de), lambda i, te, nv: (te[i], 0, 0)),
                      pl.BlockSpec((1, de, d), lambda i, te, nv: (te[i], 0, 0))],
            out_specs=pl.BlockSpec((tm, PACK_SUB, LANES), lambda i, te, nv: (i, 0, 0))),
        out_shape=jax.ShapeDtypeStruct((n_rows, PACK_SUB, LANES), U32),
        compiler_params=_params("arbitrary"),
        name="experts",
    )(tile_expert, n_valid, xs, wg_bf, wu_bf, wd_bf)


def _combine_kernel(off_ref, e_ref, rank_ref, ys_ref, x1_ref, gate_ref, g2_ref, lng_ref, lnb_ref,
                    o_ref, buf_ref, sem, *, alpha):
    tm = x1_ref.shape[0]

    def issue(r, _):
        for k in range(TOP_K):
            p = off_ref[e_ref[0, k, r]] + rank_ref[0, k, r]
            _row_copy(ys_ref, p, buf_ref.at[k], r, sem).start()
        return 0

    lax.fori_loop(0, tm, issue, 0, unroll=8)

    def drain(r, _):
        for k in range(TOP_K):
            _row_copy(ys_ref, 0, buf_ref.at[k], 0, sem).wait()
        return 0

    lax.fori_loop(0, tm, drain, 0, unroll=8)

    gates = gate_ref[...]
    y = gates[:, 0:1] * _load_packed_rows(buf_ref.at[0]) + gates[:, 1:2] * _load_packed_rows(buf_ref.at[1])
    z = alpha * x1_ref[...] + g2_ref[0] * y
    o_ref[...] = _layer_norm(z) * lng_ref[...] + lnb_ref[...]


def _combine(off, e3, rank3, ys, x1, gates, g2, ln_g, ln_b, seq, alpha):
    t, d = x1.shape
    tm = TM_DISP
    per_b = seq // tm
    smem_spec = pl.BlockSpec((1, TOP_K, tm), lambda i, *_: (i, 0, 0), memory_space=pltpu.SMEM)
    row_spec = pl.BlockSpec((1, d), lambda i, *_: (0, 0))
    return pl.pallas_call(
        functools.partial(_combine_kernel, alpha=alpha),
        grid_spec=pltpu.PrefetchScalarGridSpec(
            num_scalar_prefetch=1,
            grid=(t // tm,),
            in_specs=[smem_spec, smem_spec,
                      pl.BlockSpec(memory_space=pl.ANY),
                      pl.BlockSpec((tm, d), lambda i, *_: (i, 0)),
                      pl.BlockSpec((tm, LANES), lambda i, *_: (i, 0)),
                      pl.BlockSpec((1, 1, d), lambda i, *_: (i // per_b, 0, 0)),
                      row_spec, row_spec],
            out_specs=pl.BlockSpec((tm, d), lambda i, *_: (i, 0)),
            scratch_shapes=[pltpu.VMEM((TOP_K, tm, PACK_SUB, LANES), U32), pltpu.SemaphoreType.DMA(())]),
        out_shape=jax.ShapeDtypeStruct((t, d), F32),
        compiler_params=_params("arbitrary"),
        name="combine",
    )(off, e3, rank3, ys, x1, gates, g2, ln_g.reshape(1, d), ln_b.reshape(1, d))


def _moe(hp, eidx, gates, x1, g2, ln_g, ln_b, wg_bf, wu_bf, wd_bf, seq, alpha):
    t = hp.shape[0]
    n_slots = TOP_K * t
    n_tiles = n_slots // TM_EXP + N_EXPERTS
    rank, cnt = _ranks(eidx)
    counts = cnt[:, 0]
    tiles_e = (counts + (TM_EXP - 1)) // TM_EXP
    tile_end = jnp.cumsum(tiles_e)
    end = (tile_end * TM_EXP).astype(I32)
    off = end - (tiles_e * TM_EXP).astype(I32)
    n_valid = tile_end[-1:].astype(I32)
    tile_ids = jnp.arange(n_tiles, dtype=I32)
    tile_expert = jnp.sum(tile_ids[:, None] >= tile_end[None, :], axis=1).astype(I32)
    tile_expert = jnp.minimum(tile_expert, tile_expert[jnp.maximum(n_valid[0] - 1, 0)])
    nb = t // TM_DISP
    e3 = eidx.reshape(TOP_K, nb, TM_DISP).transpose(1, 0, 2)
    rank3 = rank.reshape(TOP_K, nb, TM_DISP).transpose(1, 0, 2)
    xs = _dispatch(off, end, counts, e3, rank3, hp, n_tiles * TM_EXP)
    ys = _experts(tile_expert, n_valid, xs, wg_bf, wu_bf, wd_bf)
    return _combine(off, e3, rank3, ys, x1, gates, g2, ln_g, ln_b, seq, alpha)


def _rotary_tables(seq):
    half = RET_QK // 2
    inv = 1.0 / (ROPE_BASE ** (jnp.arange(0, RET_QK, 2, dtype=F32) / RET_QK))
    ang = jnp.arange(seq, dtype=F32)[:, None] * inv[None, :]
    cos = jnp.cos(ang)
    sin = jnp.sin(ang)
    reps = LANES // RET_QK
    cos_t = jnp.tile(jnp.concatenate([cos, cos], axis=1), (1, reps))
    sin_t = jnp.tile(jnp.concatenate([-sin, sin], axis=1), (1, reps))
    del half
    return cos_t, sin_t


def kernel(x, c, w_ada, b_ada, w_in, w_out, lambda_q1, lambda_k1, lambda_q2, lambda_k2, diff_subln,
           ln_mix_g, ln_mix_b, ln_ffn_g, ln_ffn_b, router_w, router_b, w_gate, w_up, w_down):
    batch, seq, d = x.shape
    depth = w_ada.shape[0]
    alpha = (2 * depth) ** 0.25
    mod = _adaln(c, w_ada, b_ada)
    cos_t, sin_t = _rotary_tables(seq)
    rw_t = router_w.T.astype(BF16)
    xf = x.reshape(batch * seq, d)
    for l in range(depth):
        m = mod[l].reshape(batch, N_MOD, 1, d)
        sh1, sc1, g1, sh2, sc2, g2 = (m[:, i] for i in range(N_MOD))
        rq, rk, rv, rg, dq, dk, dv = _inproj(xf, sc1, sh1, w_in[l].astype(BF16), cos_t, sin_t, seq)
        ret = _retention(rq, rk, rv, rg, batch, seq)
        lambda_init = 0.8 - 0.6 * math.exp(-0.3 * l)
        dif = _diff_attention(dq, dk, dv, lambda_q1[l], lambda_k1[l], lambda_q2[l], lambda_k2[l],
                              diff_subln[l], lambda_init, batch, seq)
        x1, hp, eidx, gates = _outproj(ret, dif, xf, g1, sc2, sh2, w_out[l].astype(BF16),
                                       ln_mix_g[l], ln_mix_b[l], rw_t, router_b, seq, alpha)
        xf = _moe(hp, eidx, gates, x1, g2, ln_ffn_g[l], ln_ffn_b[l], w_gate[l].astype(BF16),
                  w_up[l].astype(BF16), w_down[l].astype(BF16), seq, alpha)
    return xf.reshape(batch, seq, d)
```

```python
import functools
import math

import jax
import jax.numpy as jnp
from jax import lax
from jax.experimental import pallas as pl
from jax.experimental.pallas import tpu as pltpu

F32 = jnp.float32
BF16 = jnp.bfloat16
U32 = jnp.uint32
I32 = jnp.int32

RET_HEADS = 8
RET_QK = 64
DIFF_HEADS = 4
DIFF_QK = 64
HEAD_PAIR = 128
SEG = 512
N_SEG = 7
CHUNK = 128
ROPE_BASE = 10000.0
SUBLN_EPS = 1e-5
LN_EPS = 1e-5
N_EXPERTS = 16
N_GROUPS = 4
EXPERTS_PER_GROUP = 4
TOP_K = 2
N_MOD = 6
LANES = 128
BF16_SUBLANES = 16
VMEM_LIMIT = 56 * 1024 * 1024

TM_PROJ = 512
BQ = 512
BK = 512
TS_RANK = 512
TM_DISP = 256
TM_EXP = 256


def _params(*sem):
    return pltpu.CompilerParams(dimension_semantics=sem, vmem_limit_bytes=VMEM_LIMIT)


def _layer_norm(x):
    mu = jnp.mean(x, axis=-1, keepdims=True)
    xc = x - mu
    var = jnp.mean(xc * xc, axis=-1, keepdims=True)
    return xc * lax.rsqrt(var + LN_EPS)


def _silu(x):
    return x * jax.nn.sigmoid(x)


def _adaln_kernel(c_ref, w_ref, b_ref, o_ref):
    cond = _silu(c_ref[...]).astype(BF16)
    o_ref[0] = jnp.dot(cond, w_ref[0].astype(BF16), preferred_element_type=F32) + b_ref[0]


def _adaln(c, w_ada, b_ada):
    depth, d, n = w_ada.shape
    b = c.shape[0]
    tn = 1536
    return pl.pallas_call(
        _adaln_kernel,
        grid=(depth, n // tn),
        in_specs=[pl.BlockSpec((b, d), lambda l, j: (0, 0)),
                  pl.BlockSpec((1, d, tn), lambda l, j: (l, 0, j)),
                  pl.BlockSpec((1, 1, tn), lambda l, j: (l, 0, j))],
        out_specs=pl.BlockSpec((1, b, tn), lambda l, j: (l, 0, j)),
        out_shape=jax.ShapeDtypeStruct((depth, b, n), F32),
        compiler_params=_params("arbitrary", "arbitrary"),
        name="adaln",
    )(c, w_ada, b_ada.reshape(depth, 1, n))


def _inproj_kernel(x_ref, sc_ref, sh_ref, w_ref, wvt_ref, cos_ref, sin_ref, *out_refs):
    h = (_layer_norm(x_ref[...]) * (1.0 + sc_ref[0]) + sh_ref[0]).astype(BF16)
    cos = cos_ref[...]
    sin = sin_ref[...]
    lane = lax.broadcasted_iota(I32, cos.shape, 1)
    first_half = (lane & (RET_QK - 1)) < (RET_QK // 2)

    def rotary(p):
        outs = []
        for c in range(SEG // LANES):
            pc = p[:, c * LANES:(c + 1) * LANES]
            swapped = jnp.where(first_half, pltpu.roll(pc, LANES - RET_QK // 2, 1),
                                pltpu.roll(pc, RET_QK // 2, 1))
            outs.append(pc * cos + swapped * sin)
        return jnp.concatenate(outs, axis=1)

    for s, o_ref in enumerate(out_refs[:-1]):
        p = jnp.dot(h, w_ref[:, s * SEG:(s + 1) * SEG], preferred_element_type=F32)
        if s in (0, 1):
            p = rotary(p)
        if s == 0:
            p = p * (RET_QK ** -0.5)
        if s == 4:
            p = p * (DIFF_QK ** -0.5 * math.log2(math.e))
        o_ref[...] = p.astype(BF16)
    out_refs[-1][...] = lax.dot_general(wvt_ref[...], h, (((1,), (1,)), ((), ())),
                                        preferred_element_type=F32).astype(BF16)


def _inproj(x2d, sc, sh, w_bf, cos_t, sin_t, seq):
    t, d = x2d.shape
    tm = TM_PROJ
    per_b = seq // tm
    n_main = (N_SEG - 1) * SEG
    w_main = w_bf[:, :n_main]
    w_vt = w_bf[:, n_main:].T
    return pl.pallas_call(
        _inproj_kernel,
        grid=(t // tm,),
        in_specs=[pl.BlockSpec((tm, d), lambda i: (i, 0)),
                  pl.BlockSpec((1, 1, d), lambda i: (i // per_b, 0, 0)),
                  pl.BlockSpec((1, 1, d), lambda i: (i // per_b, 0, 0)),
                  pl.BlockSpec((d, n_main), lambda i: (0, 0)),
                  pl.BlockSpec((SEG, d), lambda i: (0, 0)),
                  pl.BlockSpec((tm, LANES), lambda i: (i % per_b, 0)),
                  pl.BlockSpec((tm, LANES), lambda i: (i % per_b, 0))],
        out_specs=[pl.BlockSpec((tm, SEG), lambda i: (i, 0))] * (N_SEG - 1)
        + [pl.BlockSpec((SEG, tm), lambda i: (0, i))],
        out_shape=[jax.ShapeDtypeStruct((t, SEG), BF16)] * (N_SEG - 1)
        + [jax.ShapeDtypeStruct((SEG, t), BF16)],
        compiler_params=_params("arbitrary"),
        name="inproj",
    )(x2d, sc, sh, w_main, w_vt, cos_t, sin_t)


def _ret_kernel(q_ref, k_ref, v_ref, g_ref, d2_ref, qd_ref, kd_ref, gm_ref, o_ref, *, n_chunks):
    c = CHUNK
    lo = lax.broadcasted_iota(I32, (c, HEAD_PAIR), 1) < RET_QK
    r = lax.broadcasted_iota(I32, (HEAD_PAIR, HEAD_PAIR), 0) < RET_QK
    cc = lax.broadcasted_iota(I32, (HEAD_PAIR, HEAD_PAIR), 1) < RET_QK
    same_head = r == cc
    d2 = d2_ref[0]
    qd = qd_ref[0]
    kd = kd_ref[0]
    gm = gm_ref[0]

    def split_heads(a):
        zero = jnp.zeros_like(a)
        return jnp.concatenate([jnp.where(lo, a, zero), jnp.where(lo, zero, a)], axis=0)

    def body(n, state):
        sl = pl.ds(pl.multiple_of(n * c, c), c)
        q = q_ref[sl, :]
        k = k_ref[sl, :]
        v = v_ref[sl, :]
        s2 = lax.dot_general(split_heads(q), k, (((1,), (1,)), ((), ())), preferred_element_type=F32)
        p2 = (s2 * d2).astype(BF16)
        pcat = jnp.concatenate([p2[:c], p2[c:]], axis=1)
        inner = jnp.dot(pcat, split_heads(v), preferred_element_type=F32)
        qdq = (q.astype(F32) * qd).astype(BF16)
        cross = jnp.dot(qdq, state.astype(BF16), preferred_element_type=F32)
        o = inner + cross
        kdk = (k.astype(F32) * kd).astype(BF16)
        kv = lax.dot_general(kdk, v, (((0,), (0,)), ((), ())), preferred_element_type=F32)
        state = state * gm + jnp.where(same_head, kv, 0.0)
        inv = 1.0 / RET_QK
        s_all = jnp.sum(o, axis=-1, keepdims=True)
        s_lo = jnp.sum(jnp.where(lo, o, 0.0), axis=-1, keepdims=True)
        dlt = o - jnp.where(lo, s_lo, s_all - s_lo) * inv
        dd = dlt * dlt
        v_all = jnp.sum(dd, axis=-1, keepdims=True)
        v_lo = jnp.sum(jnp.where(lo, dd, 0.0), axis=-1, keepdims=True)
        var = jnp.where(lo, v_lo, v_all - v_lo) * inv
        y = dlt * lax.rsqrt(var + LN_EPS)
        o_ref[sl, :] = (y * _silu(g_ref[sl, :].astype(F32))).astype(BF16)
        return state

    lax.fori_loop(0, n_chunks, body, jnp.zeros((HEAD_PAIR, HEAD_PAIR), F32), unroll=8)


def _retention_tables():
    c = CHUNK
    log_g = jnp.log(1.0 - 2.0 ** (-5.0 - jnp.arange(RET_HEADS, dtype=F32)))
    idx = jnp.arange(c, dtype=F32)
    rel = idx[:, None] - idx[None, :]
    decay = jnp.where(rel >= 0, jnp.exp(log_g[:, None, None] * jnp.maximum(rel, 0.0)), 0.0)
    d2 = decay.reshape(RET_HEADS // 2, 2 * c, c)
    lane_head = jnp.arange(HEAD_PAIR) // RET_QK
    pair_log = log_g.reshape(RET_HEADS // 2, 2)[:, lane_head]
    qd = jnp.exp(pair_log[:, None, :] * (idx + 1.0)[None, :, None])
    kd = jnp.exp(pair_log[:, None, :] * (c - 1.0 - idx)[None, :, None])
    gm = jnp.broadcast_to(jnp.exp(pair_log * c)[:, :, None], (RET_HEADS // 2, HEAD_PAIR, HEAD_PAIR))
    return d2, qd, kd, gm


def _retention(rq, rk, rv, rg, batch, seq):
    t = rq.shape[0]
    n_pairs = RET_HEADS // 2
    d2, qd, kd, gm = _retention_tables()
    seq_spec = pl.BlockSpec((seq, HEAD_PAIR), lambda b, p: (b, p))
    return pl.pallas_call(
        functools.partial(_ret_kernel, n_chunks=seq // CHUNK),
        grid=(batch, n_pairs),
        in_specs=[seq_spec, seq_spec, seq_spec, seq_spec,
                  pl.BlockSpec((1, 2 * CHUNK, CHUNK), lambda b, p: (p, 0, 0)),
                  pl.BlockSpec((1, CHUNK, HEAD_PAIR), lambda b, p: (p, 0, 0)),
                  pl.BlockSpec((1, CHUNK, HEAD_PAIR), lambda b, p: (p, 0, 0)),
                  pl.BlockSpec((1, HEAD_PAIR, HEAD_PAIR), lambda b, p: (p, 0, 0))],
        out_specs=seq_spec,
        out_shape=jax.ShapeDtypeStruct((t, SEG), BF16),
        compiler_params=_params("arbitrary", "arbitrary"),
        name="retention",
    )(rq, rk, rv, rg, d2, qd, kd, gm)


def _diff_kernel(lq1_ref, lk1_ref, lq2_ref, lk2_ref, sg_ref, q_ref, k_ref, vt_ref, o_ref, vta_ref, *,
                 lambda_init):
    bq, bk = q_ref.shape[0], BK
    qi = pl.program_id(2)

    @pl.when(qi == 0)
    def _():
        vta_ref[:HEAD_PAIR, :] = vt_ref[...]
        rid = lax.broadcasted_iota(I32, (vta_ref.shape[0] - HEAD_PAIR, vta_ref.shape[1]), 0)
        vta_ref[HEAD_PAIR:, :] = jnp.where(rid == 0, 1.0, 0.0).astype(BF16)

    lam = (jnp.exp(jnp.sum(lq1_ref[...] * lk1_ref[...], axis=-1, keepdims=True))
           - jnp.exp(jnp.sum(lq2_ref[...] * lk2_ref[...], axis=-1, keepdims=True)) + lambda_init)
    q = q_ref[...]
    lo = lax.broadcasted_iota(I32, q.shape, 1) < DIFF_QK
    zero = jnp.zeros_like(q)
    q2 = jnp.concatenate([jnp.where(lo, q, zero), jnp.where(lo, zero, q)], axis=0)
    q0 = qi * bq

    def step(j, carry, masked):
        m, acc = carry
        sl = pl.ds(pl.multiple_of(j * bk, bk), bk)
        s = lax.dot_general(k_ref[sl, :], q2, (((1,), (1,)), ((), ())), preferred_element_type=F32)
        if masked:
            kpos = j * bk + lax.broadcasted_iota(I32, s.shape, 0)
            qpos = q0 + (lax.broadcasted_iota(I32, s.shape, 1) & (bq - 1))
            s = jnp.where(kpos <= qpos, s, -jnp.inf)
        m_new = jnp.maximum(m, jnp.max(s, axis=0, keepdims=True))
        p = jnp.exp2(s - m_new).astype(BF16)
        acc = jnp.exp2(m - m_new) * acc + jnp.dot(vta_ref[:, sl], p, preferred_element_type=F32)
        return m_new, acc

    init = (jnp.full((1, 2 * bq), -jnp.inf, F32), jnp.zeros((vta_ref.shape[0], 2 * bq), F32))
    n_full = lax.shift_right_logical(q0, int(math.log2(bk)))
    carry = lax.fori_loop(0, n_full, lambda j, cr: step(j, cr, False), init)
    for tail in range(bq // bk):
        carry = step(n_full + tail, carry, True)
    acc = carry[1]
    a = acc[:HEAD_PAIR] / acc[HEAD_PAIR:HEAD_PAIR + 1]
    out = (a[:, :bq] - lam * a[:, bq:]).T
    ms = jnp.mean(out * out, axis=-1, keepdims=True)
    out = out * lax.rsqrt(ms + SUBLN_EPS) * sg_ref[...] * (1.0 - lambda_init)
    o_ref[...] = out.astype(BF16)


def _diff_attention(dq, dk, dvt, lq1, lk1, lq2, lk2, subln, lambda_init, batch, seq):
    t = dq.shape[0]
    nq = seq // BQ
    lam_spec = pl.BlockSpec((1, DIFF_QK), lambda b, h, i: (0, 0))
    q_spec = pl.BlockSpec((BQ, HEAD_PAIR), lambda b, h, i: (b * nq + i, h))
    return pl.pallas_call(
        functools.partial(_diff_kernel, lambda_init=lambda_init),
        grid=(batch, DIFF_HEADS, nq),
        in_specs=[lam_spec, lam_spec, lam_spec, lam_spec,
                  pl.BlockSpec((1, HEAD_PAIR), lambda b, h, i: (0, 0)),
                  q_spec,
                  pl.BlockSpec((seq, HEAD_PAIR), lambda b, h, i: (b, h)),
                  pl.BlockSpec((HEAD_PAIR, seq), lambda b, h, i: (h, b))],
        out_specs=q_spec,
        out_shape=jax.ShapeDtypeStruct((t, SEG), BF16),
        scratch_shapes=[pltpu.VMEM((HEAD_PAIR + BF16_SUBLANES, seq), BF16)],
        compiler_params=_params("arbitrary", "arbitrary", "arbitrary"),
        name="diff_attention",
    )(lq1.reshape(1, -1), lk1.reshape(1, -1), lq2.reshape(1, -1), lk2.reshape(1, -1),
      subln.reshape(1, -1), dq, dk, dvt)


def _pack_bf16_pair(a, b):
    ua = lax.bitcast_convert_type(a.astype(BF16).astype(F32), U32)
    ub = lax.bitcast_convert_type(b.astype(BF16).astype(F32), U32)
    return (ua >> 16) | (ub & jnp.uint32(0xFFFF0000))


def _unpack_bf16_pair(w):
    a = lax.bitcast_convert_type(w << 16, F32)
    b = lax.bitcast_convert_type(w & jnp.uint32(0xFFFF0000), F32)
    return a, b


def _store_packed_rows(ref, x):
    hw = x.shape[1] // 2
    ref[...] = _pack_bf16_pair(x[:, :hw], x[:, hw:])


def _load_packed_rows(ref):
    a, b = _unpack_bf16_pair(ref[...])
    return jnp.concatenate([a, b], axis=1)


def _first_top2(vals):
    m1 = jnp.maximum(jnp.maximum(vals[0], vals[1]), jnp.maximum(vals[2], vals[3]))
    i1 = jnp.where(vals[0] == m1, 0, jnp.where(vals[1] == m1, 1, jnp.where(vals[2] == m1, 2, 3)))
    rest = [jnp.where(i1 == i, -1.0, v) for i, v in enumerate(vals)]
    m2 = jnp.maximum(jnp.maximum(rest[0], rest[1]), jnp.maximum(rest[2], rest[3]))
    i2 = jnp.where(rest[0] == m2, 0, jnp.where(rest[1] == m2, 1, jnp.where(rest[2] == m2, 2, 3)))
    return m1, i1, m2, i2


def _outproj_kernel(ret_ref, dif_ref, x_ref, g1_ref, sc2_ref, sh2_ref, wo_ref, lng_ref, lnb_ref,
                    rw_ref, rb_ref, x1_ref, hp_ref, eidx_ref, gate_ref, *, alpha):
    half = ret_ref.shape[1]
    mixed = (jnp.dot(ret_ref[...], wo_ref[:half, :], preferred_element_type=F32)
             + jnp.dot(dif_ref[...], wo_ref[half:, :], preferred_element_type=F32))
    x1 = _layer_norm(alpha * x_ref[...] + g1_ref[0] * mixed) * lng_ref[...] + lnb_ref[...]
    x1_ref[...] = x1
    h2 = _layer_norm(x1) * (1.0 + sc2_ref[0]) + sh2_ref[0]
    _store_packed_rows(hp_ref, h2)
    logits = lax.dot_general(rw_ref[...], h2.astype(BF16), (((1,), (1,)), ((), ())),
                             preferred_element_type=F32) + rb_ref[...]
    e = jnp.exp(logits - jnp.max(logits, axis=0, keepdims=True))
    probs = e / jnp.sum(e, axis=0, keepdims=True)
    rows = [probs[i:i + 1, :] for i in range(N_EXPERTS)]
    group_tops = []
    for g in range(N_GROUPS):
        m1, _, m2, _ = _first_top2(rows[g * EXPERTS_PER_GROUP:(g + 1) * EXPERTS_PER_GROUP])
        group_tops.append(m1 + m2)
    gmax = jnp.maximum(jnp.maximum(group_tops[0], group_tops[1]), jnp.maximum(group_tops[2], group_tops[3]))
    gbest = jnp.where(group_tops[0] == gmax, 0,
                      jnp.where(group_tops[1] == gmax, 1, jnp.where(group_tops[2] == gmax, 2, 3)))
    sel = []
    for i in range(EXPERTS_PER_GROUP):
        v = rows[3 * EXPERTS_PER_GROUP + i]
        for g in (2, 1, 0):
            v = jnp.where(gbest == g, rows[g * EXPERTS_PER_GROUP + i], v)
        sel.append(v)
    p0, i0, p1, i1 = _first_top2(sel)
    denom = p0 + p1
    eidx_ref[0:1, :] = gbest * EXPERTS_PER_GROUP + i0
    eidx_ref[1:2, :] = gbest * EXPERTS_PER_GROUP + i1
    tm = x1.shape[0]
    rid = lax.broadcasted_iota(I32, (LANES, tm), 0)
    gt = jnp.where(rid == 0, p0 / denom, jnp.where(rid == 1, p1 / denom, 0.0))
    gate_ref[...] = gt.T


def _outproj(ret, dif, x2d, g1, sc2, sh2, wo_bf, ln_g, ln_b, rw_t, rb, seq, alpha):
    t, d = x2d.shape
    tm = TM_PROJ
    per_b = seq // tm
    mod_spec = pl.BlockSpec((1, 1, d), lambda i: (i // per_b, 0, 0))
    row_spec = pl.BlockSpec((1, d), lambda i: (0, 0))
    return pl.pallas_call(
        functools.partial(_outproj_kernel, alpha=alpha),
        grid=(t // tm,),
        in_specs=[pl.BlockSpec((tm, SEG), lambda i: (i, 0)),
                  pl.BlockSpec((tm, SEG), lambda i: (i, 0)),
                  pl.BlockSpec((tm, d), lambda i: (i, 0)),
                  mod_spec, mod_spec, mod_spec,
                  pl.BlockSpec((2 * SEG, d), lambda i: (0, 0)),
                  row_spec, row_spec,
                  pl.BlockSpec((N_EXPERTS, d), lambda i: (0, 0)),
                  pl.BlockSpec((N_EXPERTS, 1), lambda i: (0, 0))],
        out_specs=[pl.BlockSpec((tm, d), lambda i: (i, 0)),
                   pl.BlockSpec((tm, d // 2), lambda i: (i, 0)),
                   pl.BlockSpec((TOP_K, tm), lambda i: (0, i)),
                   pl.BlockSpec((tm, LANES), lambda i: (i, 0))],
        out_shape=[jax.ShapeDtypeStruct((t, d), F32),
                   jax.ShapeDtypeStruct((t, d // 2), U32),
                   jax.ShapeDtypeStruct((TOP_K, t), I32),
                   jax.ShapeDtypeStruct((t, LANES), F32)],
        compiler_params=_params("arbitrary"),
        name="outproj_router",
    )(ret, dif, x2d, g1, sc2, sh2, wo_bf, ln_g.reshape(1, d), ln_b.reshape(1, d), rw_t, rb.reshape(-1, 1))


def _rank_kernel(e_ref, tri_ref, rank_ref, cnt_ref, carry_ref):
    i = pl.program_id(0)

    @pl.when(i == 0)
    def _():
        carry_ref[...] = jnp.zeros_like(carry_ref)

    e = e_ref[...]
    ts = e.shape[1]
    onehot = lax.broadcasted_iota(I32, (N_EXPERTS, ts), 0) == e
    cum = jnp.dot(jnp.where(onehot, 1.0, 0.0).astype(BF16), tri_ref[...],
                  preferred_element_type=F32)
    carry = carry_ref[...]
    total = cum + carry[:, 0:1]
    rank_ref[...] = (jnp.sum(jnp.where(onehot, total, 0.0), axis=0, keepdims=True) - 1.0).astype(I32)
    new_carry = carry + cum[:, ts - 1:ts]
    carry_ref[...] = new_carry
    cnt_ref[...] = new_carry.astype(I32)


def _ranks(eidx):
    n_slots = eidx.size
    ts = TS_RANK
    tri = (jnp.arange(ts)[:, None] <= jnp.arange(ts)[None, :]).astype(BF16)
    return pl.pallas_call(
        _rank_kernel,
        grid=(n_slots // ts,),
        in_specs=[pl.BlockSpec((1, ts), lambda i: (0, i)),
                  pl.BlockSpec((ts, ts), lambda i: (0, 0))],
        out_specs=[pl.BlockSpec((1, ts), lambda i: (0, i)),
                   pl.BlockSpec((N_EXPERTS, LANES), lambda i: (0, 0))],
        out_shape=[jax.ShapeDtypeStruct((1, n_slots), I32),
                   jax.ShapeDtypeStruct((N_EXPERTS, LANES), I32)],
        scratch_shapes=[pltpu.VMEM((N_EXPERTS, LANES), F32)],
        compiler_params=_params("arbitrary"),
        name="slot_ranks",
    )(eidx.reshape(1, n_slots), tri)


def _pos_kernel(off_ref, e_ref, rank_ref, pos_ref):
    e = e_ref[...]
    pos = rank_ref[...]
    for k in range(N_EXPERTS):
        pos = pos + jnp.where(e == k, off_ref[k], 0)
    pos_ref[...] = pos


def _positions(off, e_flat, rank):
    n_slots = rank.shape[1]
    ts = min(n_slots, 8192)
    spec = pl.BlockSpec((1, ts), lambda i, *_: (0, i))
    return pl.pallas_call(
        _pos_kernel,
        grid_spec=pltpu.PrefetchScalarGridSpec(
            num_scalar_prefetch=1, grid=(n_slots // ts,), in_specs=[spec, spec], out_specs=spec),
        out_shape=jax.ShapeDtypeStruct((1, n_slots), I32),
        compiler_params=_params("arbitrary"),
        name="slot_positions",
    )(off, e_flat, rank)


def _row_copy(src_ref, r, dst_ref, p, sem):
    return pltpu.make_async_copy(src_ref.at[pl.ds(r, 1)], dst_ref.at[pl.ds(p, 1)], sem)


def _rows_wait(src_ref, dst_ref, sem):
    pltpu.make_async_copy(src_ref, dst_ref, sem).wait()


def _dispatch_kernel(end_ref, cnt_ref, pos_ref, hp_ref, xs_ref, zero_ref, sem, zsem):
    i = pl.program_id(0)
    tm = hp_ref.shape[0]
    tz = zero_ref.shape[0]

    @pl.when(i == 0)
    def _():
        zero_ref[...] = jnp.zeros_like(zero_ref)
        n_rows = xs_ref.shape[0]
        used = end_ref[N_EXPERTS - 1]

        def clear(start):
            return pltpu.make_async_copy(zero_ref, xs_ref.at[pl.ds(pl.multiple_of(start, tz), tz)], zsem)

        for wait in (False, True):
            for e in range(N_EXPERTS):
                @pl.when(cnt_ref[e] > 0)
                def _():
                    cp = clear(end_ref[e] - tz)
                    cp.wait() if wait else cp.start()

                @pl.when(used + e * tz < n_rows)
                def _():
                    cp = clear(used + e * tz)
                    cp.wait() if wait else cp.start()

    def issue(r, _):
        for k in range(TOP_K):
            _row_copy(hp_ref, r, xs_ref, pos_ref[0, k, r], sem).start(priority=k)
        return 0

    lax.fori_loop(0, tm, issue, 0, unroll=8)
    for k in range(TOP_K):
        _rows_wait(hp_ref, xs_ref.at[pl.ds(0, tm)], sem)


def _dispatch(end, cnt, pos3, hp, n_rows):
    t, w = hp.shape
    tm = TM_DISP
    return pl.pallas_call(
        _dispatch_kernel,
        grid_spec=pltpu.PrefetchScalarGridSpec(
            num_scalar_prefetch=2,
            grid=(t // tm,),
            in_specs=[pl.BlockSpec((1, TOP_K, tm), lambda i, *_: (i, 0, 0), memory_space=pltpu.SMEM),
                      pl.BlockSpec((tm, w), lambda i, *_: (i, 0))],
            out_specs=pl.BlockSpec(memory_space=pl.ANY),
            scratch_shapes=[pltpu.VMEM((TM_EXP, w), U32), pltpu.SemaphoreType.DMA(()),
                            pltpu.SemaphoreType.DMA(())]),
        out_shape=jax.ShapeDtypeStruct((n_rows, w), U32),
        compiler_params=_params("arbitrary"),
        name="dispatch",
    )(end, cnt, pos3, hp)


def _expert_kernel(te_ref, nv_ref, xs_ref, wg_ref, wu_ref, wd_ref, ys_ref):
    i = pl.program_id(0)

    @pl.when(i < nv_ref[0])
    def _():
        x = _load_packed_rows(xs_ref).astype(BF16)
        g = jnp.dot(x, wg_ref[0], preferred_element_type=F32)
        u = jnp.dot(x, wu_ref[0], preferred_element_type=F32)
        he = (_silu(g) * u).astype(BF16)
        _store_packed_rows(ys_ref, jnp.dot(he, wd_ref[0], preferred_element_type=F32))

    @pl.when(i >= nv_ref[0])
    def _():
        ys_ref[...] = jnp.zeros_like(ys_ref)


def _experts(tile_expert, n_valid, xs, wg_bf, wu_bf, wd_bf):
    n_rows, w = xs.shape
    tm = TM_EXP
    _, d, de = wg_bf.shape

    def row_map(i, te, nv):
        return (jnp.minimum(i, nv[0] - 1), 0)

    return pl.pallas_call(
        _expert_kernel,
        grid_spec=pltpu.PrefetchScalarGridSpec(
            num_scalar_prefetch=2,
            grid=(n_rows // tm,),
            in_specs=[pl.BlockSpec((tm, w), row_map),
                      pl.BlockSpec((1, d, de), lambda i, te, nv: (te[i], 0, 0)),
                      pl.BlockSpec((1, d, de), lambda i, te, nv: (te[i], 0, 0)),
                      pl.BlockSpec((1, de, d), lambda i, te, nv: (te[i], 0, 0))],
            out_specs=pl.BlockSpec((tm, w), lambda i, te, nv: (i, 0))),
        out_shape=jax.ShapeDtypeStruct((n_rows, w), U32),
        compiler_params=_params("arbitrary"),
        name="experts",
    )(tile_expert, n_valid, xs, wg_bf, wu_bf, wd_bf)


def _combine_kernel(pos_ref, pos_next_ref, ys_ref, x1_ref, gate_ref, g2_ref, lng_ref, lnb_ref,
                    o_ref, buf_ref, sems, *, alpha):
    i = pl.program_id(0)
    tm = x1_ref.shape[0]
    slot = lax.rem(i, 2)

    def gather(p_ref, s):
        def issue(r, _):
            for k in range(TOP_K):
                _row_copy(ys_ref, p_ref[0, k, r], buf_ref.at[s, k], r, sems.at[s]).start(priority=k)
            return 0

        lax.fori_loop(0, tm, issue, 0, unroll=8)

    @pl.when(i == 0)
    def _():
        gather(pos_ref, 0)

    @pl.when(i + 1 < pl.num_programs(0))
    def _():
        gather(pos_next_ref, 1 - slot)

    for k in range(TOP_K):
        _rows_wait(ys_ref.at[pl.ds(0, tm)], buf_ref.at[slot, k], sems.at[slot])
    gates = gate_ref[...]
    y = (gates[:, 0:1] * _load_packed_rows(buf_ref.at[slot, 0])
         + gates[:, 1:2] * _load_packed_rows(buf_ref.at[slot, 1]))
    z = alpha * x1_ref[...] + g2_ref[0] * y
    o_ref[...] = _layer_norm(z) * lng_ref[...] + lnb_ref[...]


def _combine(pos3, ys, x1, gates, g2, ln_g, ln_b, seq, alpha):
    t, d = x1.shape
    w = ys.shape[1]
    tm = TM_DISP
    nb = t // tm
    per_b = seq // tm
    row_spec = pl.BlockSpec((1, d), lambda i: (0, 0))
    return pl.pallas_call(
        functools.partial(_combine_kernel, alpha=alpha),
        grid=(nb,),
        in_specs=[pl.BlockSpec((1, TOP_K, tm), lambda i: (i, 0, 0), memory_space=pltpu.SMEM),
                  pl.BlockSpec((1, TOP_K, tm), lambda i: (jnp.minimum(i + 1, nb - 1), 0, 0),
                               memory_space=pltpu.SMEM),
                  pl.BlockSpec(memory_space=pl.ANY),
                  pl.BlockSpec((tm, d), lambda i: (i, 0)),
                  pl.BlockSpec((tm, LANES), lambda i: (i, 0)),
                  pl.BlockSpec((1, 1, d), lambda i: (i // per_b, 0, 0)),
                  row_spec, row_spec],
        out_specs=pl.BlockSpec((tm, d), lambda i: (i, 0)),
        out_shape=jax.ShapeDtypeStruct((t, d), F32),
        scratch_shapes=[pltpu.VMEM((2, TOP_K, tm, w), U32), pltpu.SemaphoreType.DMA((2,))],
        compiler_params=_params("arbitrary"),
        name="combine",
    )(pos3, pos3, ys, x1, gates, g2, ln_g.reshape(1, d), ln_b.reshape(1, d))


def _moe(hp, eidx, gates, x1, g2, ln_g, ln_b, wg_bf, wu_bf, wd_bf, seq, alpha):
    t = hp.shape[0]
    n_slots = TOP_K * t
    n_tiles = n_slots // TM_EXP + N_EXPERTS
    rank, cnt = _ranks(eidx)
    counts = cnt[:, 0]
    tiles_e = (counts + (TM_EXP - 1)) // TM_EXP
    tile_end = jnp.cumsum(tiles_e)
    end = (tile_end * TM_EXP).astype(I32)
    off = end - (tiles_e * TM_EXP).astype(I32)
    n_valid = tile_end[-1:].astype(I32)
    tile_ids = jnp.arange(n_tiles, dtype=I32)
    tile_expert = jnp.sum(tile_ids[:, None] >= tile_end[None, :], axis=1).astype(I32)
    tile_expert = jnp.minimum(tile_expert, tile_expert[jnp.maximum(n_valid[0] - 1, 0)])
    nb = t // TM_DISP
    pos = _positions(off, eidx.reshape(1, n_slots), rank)
    pos3 = pos.reshape(TOP_K, nb, TM_DISP).transpose(1, 0, 2)
    xs = _dispatch(end, counts, pos3, hp, n_tiles * TM_EXP)
    ys = _experts(tile_expert, n_valid, xs, wg_bf, wu_bf, wd_bf)
    return _combine(pos3, ys, x1, gates, g2, ln_g, ln_b, seq, alpha)


def _rotary_tables(seq):
    half = RET_QK // 2
    inv = 1.0 / (ROPE_BASE ** (jnp.arange(0, RET_QK, 2, dtype=F32) / RET_QK))
    ang = jnp.arange(seq, dtype=F32)[:, None] * inv[None, :]
    cos = jnp.cos(ang)
    sin = jnp.sin(ang)
    reps = LANES // RET_QK
    cos_t = jnp.tile(jnp.concatenate([cos, cos], axis=1), (1, reps))
    sin_t = jnp.tile(jnp.concatenate([-sin, sin], axis=1), (1, reps))
    del half
    return cos_t, sin_t


def kernel(x, c, w_ada, b_ada, w_in, w_out, lambda_q1, lambda_k1, lambda_q2, lambda_k2, diff_subln,
           ln_mix_g, ln_mix_b, ln_ffn_g, ln_ffn_b, router_w, router_b, w_gate, w_up, w_down):
    batch, seq, d = x.shape
    depth = w_ada.shape[0]
    alpha = (2 * depth) ** 0.25
    mod = _adaln(c, w_ada, b_ada)
    cos_t, sin_t = _rotary_tables(seq)
    rw_t = router_w.T.astype(BF16)
    xf = x.reshape(batch * seq, d)
    for l in range(depth):
        m = mod[l].reshape(batch, N_MOD, 1, d)
        sh1, sc1, g1, sh2, sc2, g2 = (m[:, i] for i in range(N_MOD))
        rq, rk, rv, rg, dq, dk, dvt = _inproj(xf, sc1, sh1, w_in[l].astype(BF16), cos_t, sin_t, seq)
        ret = _retention(rq, rk, rv, rg, batch, seq)
        lambda_init = 0.8 - 0.6 * math.exp(-0.3 * l)
        dif = _diff_attention(dq, dk, dvt, lambda_q1[l], lambda_k1[l], lambda_q2[l], lambda_k2[l],
                              diff_subln[l], lambda_init, batch, seq)
        x1, hp, eidx, gates = _outproj(ret, dif, xf, g1, sc2, sh2, w_out[l].astype(BF16),
                                       ln_mix_g[l], ln_mix_b[l], rw_t, router_b, seq, alpha)
        xf = _moe(hp, eidx, gates, x1, g2, ln_ffn_g[l], ln_ffn_b[l], w_gate[l].astype(BF16),
                  w_up[l].astype(BF16), w_down[l].astype(BF16), seq, alpha)
    return xf.reshape(batch, seq, d)
```

```python
import functools
import math

import jax
import jax.numpy as jnp
from jax import lax
from jax.experimental import pallas as pl
from jax.experimental.pallas import tpu as pltpu

F32 = jnp.float32
BF16 = jnp.bfloat16
U32 = jnp.uint32
I32 = jnp.int32

RET_HEADS = 8
RET_QK = 64
DIFF_HEADS = 4
DIFF_QK = 64
HEAD_PAIR = 128
SEG = 512
N_SEG = 7
CHUNK = 128
ROPE_BASE = 10000.0
SUBLN_EPS = 1e-5
LN_EPS = 1e-5
N_EXPERTS = 16
N_GROUPS = 4
EXPERTS_PER_GROUP = 4
TOP_K = 2
GROUP_PAIRS = tuple((i, j) for i in range(EXPERTS_PER_GROUP) for j in range(i + 1, EXPERTS_PER_GROUP))
N_CLASSES = N_GROUPS * len(GROUP_PAIRS)
CLASS_ROWS = 32
N_MOD = 6
LANES = 128
BF16_SUBLANES = 16
VMEM_LIMIT = 56 * 1024 * 1024

TM_PROJ = 512
RET_GROUP = 8
CT = 512
BK = 512
TS_RANK = 512
TM_DISP = 256
TM_EXP = 512


def _params(*sem):
    return pltpu.CompilerParams(dimension_semantics=sem, vmem_limit_bytes=VMEM_LIMIT)


def _layer_norm(x):
    mu = jnp.mean(x, axis=-1, keepdims=True)
    xc = x - mu
    var = jnp.mean(xc * xc, axis=-1, keepdims=True)
    return xc * lax.rsqrt(var + LN_EPS)


def _silu(x):
    return x * jax.nn.sigmoid(x)


def _adaln_kernel(c_ref, w_ref, b_ref, o_ref):
    cond = _silu(c_ref[...]).astype(BF16)
    o_ref[0] = jnp.dot(cond, w_ref[0].astype(BF16), preferred_element_type=F32) + b_ref[0]


def _adaln(c, w_ada, b_ada):
    depth, d, n = w_ada.shape
    b = c.shape[0]
    tn = 1536
    return pl.pallas_call(
        _adaln_kernel,
        grid=(depth, n // tn),
        in_specs=[pl.BlockSpec((b, d), lambda l, j: (0, 0)),
                  pl.BlockSpec((1, d, tn), lambda l, j: (l, 0, j)),
                  pl.BlockSpec((1, 1, tn), lambda l, j: (l, 0, j))],
        out_specs=pl.BlockSpec((1, b, tn), lambda l, j: (l, 0, j)),
        out_shape=jax.ShapeDtypeStruct((depth, b, n), F32),
        compiler_params=_params("arbitrary", "arbitrary"),
        name="adaln",
    )(c, w_ada, b_ada.reshape(depth, 1, n))


def _inproj_kernel(x_ref, sc_ref, sh_ref, w_ref, wvt_ref, cos_ref, sin_ref, *out_refs):
    h = (_layer_norm(x_ref[...]) * (1.0 + sc_ref[0]) + sh_ref[0]).astype(BF16)
    cos = cos_ref[...]
    sin = sin_ref[...]
    lane = lax.broadcasted_iota(I32, cos.shape, 1)
    first_half = (lane & (RET_QK - 1)) < (RET_QK // 2)

    def rotary(p):
        outs = []
        for c in range(SEG // LANES):
            pc = p[:, c * LANES:(c + 1) * LANES]
            swapped = jnp.where(first_half, pltpu.roll(pc, LANES - RET_QK // 2, 1),
                                pltpu.roll(pc, RET_QK // 2, 1))
            outs.append(pc * cos + swapped * sin)
        return jnp.concatenate(outs, axis=1)

    for s, o_ref in enumerate(out_refs[:-1]):
        p = jnp.dot(h, w_ref[:, s * SEG:(s + 1) * SEG], preferred_element_type=F32)
        if s in (0, 1):
            p = rotary(p)
        if s == 0:
            p = p * (RET_QK ** -0.5)
        if s == 4:
            p = p * (DIFF_QK ** -0.5 * math.log2(math.e))
        o_ref[...] = p.astype(BF16)
    out_refs[-1][...] = lax.dot_general(wvt_ref[...], h, (((1,), (1,)), ((), ())),
                                        preferred_element_type=F32).astype(BF16)


def _inproj(x2d, sc, sh, w_bf, cos_t, sin_t, seq):
    t, d = x2d.shape
    tm = TM_PROJ
    per_b = seq // tm
    n_main = (N_SEG - 1) * SEG
    w_main = w_bf[:, :n_main]
    w_vt = w_bf[:, n_main:].T
    return pl.pallas_call(
        _inproj_kernel,
        grid=(t // tm,),
        in_specs=[pl.BlockSpec((tm, d), lambda i: (i, 0)),
                  pl.BlockSpec((1, 1, d), lambda i: (i // per_b, 0, 0)),
                  pl.BlockSpec((1, 1, d), lambda i: (i // per_b, 0, 0)),
                  pl.BlockSpec((d, n_main), lambda i: (0, 0)),
                  pl.BlockSpec((SEG, d), lambda i: (0, 0)),
                  pl.BlockSpec((tm, LANES), lambda i: (i % per_b, 0)),
                  pl.BlockSpec((tm, LANES), lambda i: (i % per_b, 0))],
        out_specs=[pl.BlockSpec((tm, SEG), lambda i: (i, 0))] * (N_SEG - 1)
        + [pl.BlockSpec((SEG, tm), lambda i: (0, i))],
        out_shape=[jax.ShapeDtypeStruct((t, SEG), BF16)] * (N_SEG - 1)
        + [jax.ShapeDtypeStruct((SEG, t), BF16)],
        compiler_params=_params("arbitrary"),
        name="inproj",
    )(x2d, sc, sh, w_main, w_vt, cos_t, sin_t)


def _ret_kernel(q_ref, k_ref, v_ref, g_ref, d2_ref, qd_ref, kd_ref, gm_ref, o_ref, *, n_chunks):
    c = CHUNK
    lo = lax.broadcasted_iota(I32, (c, HEAD_PAIR), 1) < RET_QK
    r = lax.broadcasted_iota(I32, (HEAD_PAIR, HEAD_PAIR), 0) < RET_QK
    cc = lax.broadcasted_iota(I32, (HEAD_PAIR, HEAD_PAIR), 1) < RET_QK
    same_head = r == cc
    d2 = d2_ref[0]
    qd = qd_ref[0]
    kd = kd_ref[0]
    gm = gm_ref[0]

    def split_heads(a):
        zero = jnp.zeros_like(a)
        return jnp.concatenate([jnp.where(lo, a, zero), jnp.where(lo, zero, a)], axis=0)

    grp = RET_GROUP

    def body(t, state):
        sls = [pl.ds(pl.multiple_of((t * grp + g) * c, c), c) for g in range(grp)]
        qs = [q_ref[sl, :] for sl in sls]
        ks = [k_ref[sl, :] for sl in sls]
        vs = [v_ref[sl, :] for sl in sls]
        s2s = [lax.dot_general(split_heads(q), k, (((1,), (1,)), ((), ())), preferred_element_type=F32)
               for q, k in zip(qs, ks)]
        kvs = [lax.dot_general((k.astype(F32) * kd).astype(BF16), v, (((0,), (0,)), ((), ())),
                               preferred_element_type=F32) for k, v in zip(ks, vs)]
        states = []
        for kv in kvs:
            states.append(state)
            state = state * gm + jnp.where(same_head, kv, 0.0)
        outs = []
        for q, v, s2, st in zip(qs, vs, s2s, states):
            p2 = (s2 * d2).astype(BF16)
            pcat = jnp.concatenate([p2[:c], p2[c:]], axis=1)
            inner = jnp.dot(pcat, split_heads(v), preferred_element_type=F32)
            qdq = (q.astype(F32) * qd).astype(BF16)
            outs.append(inner + jnp.dot(qdq, st.astype(BF16), preferred_element_type=F32))
        inv = 1.0 / RET_QK
        for sl, o in zip(sls, outs):
            s_all = jnp.sum(o, axis=-1, keepdims=True)
            s_lo = jnp.sum(jnp.where(lo, o, 0.0), axis=-1, keepdims=True)
            dlt = o - jnp.where(lo, s_lo, s_all - s_lo) * inv
            dd = dlt * dlt
            v_all = jnp.sum(dd, axis=-1, keepdims=True)
            v_lo = jnp.sum(jnp.where(lo, dd, 0.0), axis=-1, keepdims=True)
            var = jnp.where(lo, v_lo, v_all - v_lo) * inv
            y = dlt * lax.rsqrt(var + LN_EPS)
            o_ref[sl, :] = (y * _silu(g_ref[sl, :].astype(F32))).astype(BF16)
        return state

    lax.fori_loop(0, n_chunks // grp, body, jnp.zeros((HEAD_PAIR, HEAD_PAIR), F32))


def _retention_tables():
    c = CHUNK
    log_g = jnp.log(1.0 - 2.0 ** (-5.0 - jnp.arange(RET_HEADS, dtype=F32)))
    idx = jnp.arange(c, dtype=F32)
    rel = idx[:, None] - idx[None, :]
    decay = jnp.where(rel >= 0, jnp.exp(log_g[:, None, None] * jnp.maximum(rel, 0.0)), 0.0)
    d2 = decay.reshape(RET_HEADS // 2, 2 * c, c)
    lane_head = jnp.arange(HEAD_PAIR) // RET_QK
    pair_log = log_g.reshape(RET_HEADS // 2, 2)[:, lane_head]
    qd = jnp.exp(pair_log[:, None, :] * (idx + 1.0)[None, :, None])
    kd = jnp.exp(pair_log[:, None, :] * (c - 1.0 - idx)[None, :, None])
    gm = jnp.broadcast_to(jnp.exp(pair_log * c)[:, :, None], (RET_HEADS // 2, HEAD_PAIR, HEAD_PAIR))
    return d2, qd, kd, gm


def _retention(rq, rk, rv, rg, batch, seq):
    t = rq.shape[0]
    n_pairs = RET_HEADS // 2
    d2, qd, kd, gm = _retention_tables()
    seq_spec = pl.BlockSpec((seq, HEAD_PAIR), lambda b, p: (b, p))
    return pl.pallas_call(
        functools.partial(_ret_kernel, n_chunks=seq // CHUNK),
        grid=(batch, n_pairs),
        in_specs=[seq_spec, seq_spec, seq_spec, seq_spec,
                  pl.BlockSpec((1, 2 * CHUNK, CHUNK), lambda b, p: (p, 0, 0)),
                  pl.BlockSpec((1, CHUNK, HEAD_PAIR), lambda b, p: (p, 0, 0)),
                  pl.BlockSpec((1, CHUNK, HEAD_PAIR), lambda b, p: (p, 0, 0)),
                  pl.BlockSpec((1, HEAD_PAIR, HEAD_PAIR), lambda b, p: (p, 0, 0))],
        out_specs=seq_spec,
        out_shape=jax.ShapeDtypeStruct((t, SEG), BF16),
        compiler_params=_params("arbitrary", "arbitrary"),
        name="retention",
    )(rq, rk, rv, rg, d2, qd, kd, gm)


def _diff_kernel(lq1_ref, lk1_ref, lq2_ref, lk2_ref, sg_ref, q_ref, k_ref, vt_ref, o_ref,
                 vta_ref, sa_ref, sb_ref, mx_ref, m_ref, acc_ref, *, lambda_init):
    bq, bk, ct = q_ref.shape[0], BK, CT
    n_ct = 2 * bq // ct
    s_refs = (sa_ref, sb_ref)

    vta_ref[:HEAD_PAIR, :] = vt_ref[...]
    rid = lax.broadcasted_iota(I32, (vta_ref.shape[0] - HEAD_PAIR, vta_ref.shape[1]), 0)
    vta_ref[HEAD_PAIR:, :] = jnp.where(rid == 0, 1.0, 0.0).astype(BF16)

    lam = (jnp.exp(jnp.sum(lq1_ref[...] * lk1_ref[...], axis=-1, keepdims=True))
           - jnp.exp(jnp.sum(lq2_ref[...] * lk2_ref[...], axis=-1, keepdims=True)) + lambda_init)
    q = q_ref[...]
    lo = lax.broadcasted_iota(I32, q.shape, 1) < DIFF_QK
    zero = jnp.zeros_like(q)
    q2 = jnp.concatenate([jnp.where(lo, q, zero), jnp.where(lo, zero, q)], axis=0)
    q_tiles = [q2[c * ct:(c + 1) * ct, :] for c in range(n_ct)]

    def block_modes(j):
        modes = []
        for c in range(n_ct):
            q_lo = (c * ct) % bq
            if j * bk >= q_lo + ct:
                modes.append("skip")
            elif (j + 1) * bk - 1 <= q_lo:
                modes.append("full")
            else:
                modes.append("masked")
        return tuple(modes)

    def scores(j, slot, modes):
        k_blk = k_ref[j * bk:(j + 1) * bk, :]
        for c, qt in enumerate(q_tiles):
            if modes[c] == "skip":
                continue
            cols = slice(c * ct, (c + 1) * ct)
            s = lax.dot_general(k_blk, qt, (((1,), (1,)), ((), ())), preferred_element_type=F32)
            if modes[c] == "masked":
                kpos = j * bk + lax.broadcasted_iota(I32, s.shape, 0)
                qpos = (c * ct) % bq + lax.broadcasted_iota(I32, s.shape, 1)
                s = jnp.where(kpos <= qpos, s, -jnp.inf)
            s_refs[slot][:, cols] = s
            mx_ref[slot:slot + 1, cols] = jnp.max(s, axis=0, keepdims=True)

    def update(j, slot, modes):
        vta_blk = vta_ref[:, j * bk:(j + 1) * bk]
        for c in range(n_ct):
            if modes[c] == "skip":
                continue
            cols = slice(c * ct, (c + 1) * ct)
            m = m_ref[:, cols]
            m_new = jnp.maximum(m, mx_ref[slot:slot + 1, cols])
            p = jnp.exp2(s_refs[slot][:, cols] - m_new).astype(BF16)
            acc_ref[:, cols] = (jnp.exp2(m - m_new) * acc_ref[:, cols]
                                + jnp.dot(vta_blk, p, preferred_element_type=F32))
            m_ref[:, cols] = m_new

    m_ref[...] = jnp.full(m_ref.shape, -jnp.inf, F32)
    acc_ref[...] = jnp.zeros(acc_ref.shape, F32)
    n_blocks = bq // bk
    scores(0, 0, block_modes(0))
    for j in range(n_blocks):
        if j + 1 < n_blocks:
            scores(j + 1, (j + 1) % 2, block_modes(j + 1))
        update(j, j % 2, block_modes(j))
    acc = acc_ref[...]
    a = acc[:HEAD_PAIR] / acc[HEAD_PAIR:HEAD_PAIR + 1]
    out = (a[:, :bq] - lam * a[:, bq:]).T
    ms = jnp.mean(out * out, axis=-1, keepdims=True)
    out = out * lax.rsqrt(ms + SUBLN_EPS) * sg_ref[...] * (1.0 - lambda_init)
    o_ref[...] = out.astype(BF16)


def _diff_attention(dq, dk, dvt, lq1, lk1, lq2, lk2, subln, lambda_init, batch, seq):
    t = dq.shape[0]
    lam_spec = pl.BlockSpec((1, DIFF_QK), lambda b, h: (0, 0))
    seq_spec = pl.BlockSpec((seq, HEAD_PAIR), lambda b, h: (b, h))
    n_rows = HEAD_PAIR + BF16_SUBLANES
    return pl.pallas_call(
        functools.partial(_diff_kernel, lambda_init=lambda_init),
        grid=(batch, DIFF_HEADS),
        in_specs=[lam_spec, lam_spec, lam_spec, lam_spec,
                  pl.BlockSpec((1, HEAD_PAIR), lambda b, h: (0, 0)),
                  seq_spec, seq_spec,
                  pl.BlockSpec((HEAD_PAIR, seq), lambda b, h: (h, b))],
        out_specs=seq_spec,
        out_shape=jax.ShapeDtypeStruct((t, SEG), BF16),
        scratch_shapes=[pltpu.VMEM((n_rows, seq), BF16),
                        pltpu.VMEM((BK, 2 * seq), F32), pltpu.VMEM((BK, 2 * seq), F32),
                        pltpu.VMEM((2, 2 * seq), F32), pltpu.VMEM((1, 2 * seq), F32),
                        pltpu.VMEM((n_rows, 2 * seq), F32)],
        compiler_params=_params("arbitrary", "arbitrary"),
        name="diff_attention",
    )(lq1.reshape(1, -1), lk1.reshape(1, -1), lq2.reshape(1, -1), lk2.reshape(1, -1),
      subln.reshape(1, -1), dq, dk, dvt)


def _pack_bf16_pair(a, b):
    ua = lax.bitcast_convert_type(a.astype(BF16).astype(F32), U32)
    ub = lax.bitcast_convert_type(b.astype(BF16).astype(F32), U32)
    return (ua >> 16) | (ub & jnp.uint32(0xFFFF0000))


def _unpack_bf16_pair(w):
    a = lax.bitcast_convert_type(w << 16, F32)
    b = lax.bitcast_convert_type(w & jnp.uint32(0xFFFF0000), F32)
    return a, b


def _store_packed_rows(ref, x):
    hw = x.shape[1] // 2
    ref[:, :hw] = _pack_bf16_pair(x[:, :hw], x[:, hw:])


def _load_packed_rows(ref, hw):
    a, b = _unpack_bf16_pair(ref[:, :hw])
    return jnp.concatenate([a, b], axis=1)


def _first_top2(vals):
    m1 = jnp.maximum(jnp.maximum(vals[0], vals[1]), jnp.maximum(vals[2], vals[3]))
    i1 = jnp.where(vals[0] == m1, 0, jnp.where(vals[1] == m1, 1, jnp.where(vals[2] == m1, 2, 3)))
    rest = [jnp.where(i1 == i, -1.0, v) for i, v in enumerate(vals)]
    m2 = jnp.maximum(jnp.maximum(rest[0], rest[1]), jnp.maximum(rest[2], rest[3]))
    i2 = jnp.where(rest[0] == m2, 0, jnp.where(rest[1] == m2, 1, jnp.where(rest[2] == m2, 2, 3)))
    return m1, i1, m2, i2


def _outproj_kernel(ret_ref, dif_ref, x_ref, g1_ref, sc2_ref, sh2_ref, wo_ref, lng_ref, lnb_ref,
                    rw_ref, rb_ref, x1_ref, hp_ref, cls_ref, *, alpha):
    half = ret_ref.shape[1]
    hw = x_ref.shape[1] // 2
    mixed = (jnp.dot(ret_ref[...], wo_ref[:half, :], preferred_element_type=F32)
             + jnp.dot(dif_ref[...], wo_ref[half:, :], preferred_element_type=F32))
    x1 = _layer_norm(alpha * x_ref[...] + g1_ref[0] * mixed) * lng_ref[...] + lnb_ref[...]
    x1_ref[...] = x1
    h2 = _layer_norm(x1) * (1.0 + sc2_ref[0]) + sh2_ref[0]
    _store_packed_rows(hp_ref, h2)
    logits = lax.dot_general(rw_ref[...], h2.astype(BF16), (((1,), (1,)), ((), ())),
                             preferred_element_type=F32) + rb_ref[...]
    e = jnp.exp(logits - jnp.max(logits, axis=0, keepdims=True))
    probs = e / jnp.sum(e, axis=0, keepdims=True)
    rows = [probs[i:i + 1, :] for i in range(N_EXPERTS)]
    group_tops = []
    for g in range(N_GROUPS):
        m1, _, m2, _ = _first_top2(rows[g * EXPERTS_PER_GROUP:(g + 1) * EXPERTS_PER_GROUP])
        group_tops.append(m1 + m2)
    gmax = jnp.maximum(jnp.maximum(group_tops[0], group_tops[1]), jnp.maximum(group_tops[2], group_tops[3]))
    gbest = jnp.where(group_tops[0] == gmax, 0,
                      jnp.where(group_tops[1] == gmax, 1, jnp.where(group_tops[2] == gmax, 2, 3)))
    sel = []
    for i in range(EXPERTS_PER_GROUP):
        v = rows[3 * EXPERTS_PER_GROUP + i]
        for g in (2, 1, 0):
            v = jnp.where(gbest == g, rows[g * EXPERTS_PER_GROUP + i], v)
        sel.append(v)
    p0, i0, p1, i1 = _first_top2(sel)
    denom = p0 + p1
    first_is_low = i0 < i1
    i_lo = jnp.where(first_is_low, i0, i1)
    i_hi = jnp.where(first_is_low, i1, i0)
    pair_base = jnp.where(i_lo == 0, 0, jnp.where(i_lo == 1, 3, 5))
    cls_ref[...] = gbest * len(GROUP_PAIRS) + pair_base + (i_hi - i_lo - 1)
    g_lo = jnp.where(first_is_low, p0, p1) / denom
    g_hi = jnp.where(first_is_low, p1, p0) / denom
    tm = x1.shape[0]
    rid = lax.broadcasted_iota(I32, (LANES, tm), 0)
    gt = jnp.where(rid == 0, g_lo, jnp.where(rid == 1, g_hi, 0.0))
    hp_ref[:, hw:] = lax.bitcast_convert_type(gt.T, U32)


def _outproj(ret, dif, x2d, g1, sc2, sh2, wo_bf, ln_g, ln_b, rw_t, rb, seq, alpha):
    t, d = x2d.shape
    tm = TM_PROJ
    per_b = seq // tm
    mod_spec = pl.BlockSpec((1, 1, d), lambda i: (i // per_b, 0, 0))
    row_spec = pl.BlockSpec((1, d), lambda i: (0, 0))
    return pl.pallas_call(
        functools.partial(_outproj_kernel, alpha=alpha),
        grid=(t // tm,),
        in_specs=[pl.BlockSpec((tm, SEG), lambda i: (i, 0)),
                  pl.BlockSpec((tm, SEG), lambda i: (i, 0)),
                  pl.BlockSpec((tm, d), lambda i: (i, 0)),
                  mod_spec, mod_spec, mod_spec,
                  pl.BlockSpec((2 * SEG, d), lambda i: (0, 0)),
                  row_spec, row_spec,
                  pl.BlockSpec((N_EXPERTS, d), lambda i: (0, 0)),
                  pl.BlockSpec((N_EXPERTS, 1), lambda i: (0, 0))],
        out_specs=[pl.BlockSpec((tm, d), lambda i: (i, 0)),
                   pl.BlockSpec((tm, d // 2 + LANES), lambda i: (i, 0)),
                   pl.BlockSpec((1, tm), lambda i: (0, i))],
        out_shape=[jax.ShapeDtypeStruct((t, d), F32),
                   jax.ShapeDtypeStruct((t, d // 2 + LANES), U32),
                   jax.ShapeDtypeStruct((1, t), I32)],
        compiler_params=_params("arbitrary"),
        name="outproj_router",
    )(ret, dif, x2d, g1, sc2, sh2, wo_bf, ln_g.reshape(1, d), ln_b.reshape(1, d), rw_t, rb.reshape(-1, 1))


def _rank_kernel(e_ref, tri_ref, rank_ref, cnt_ref, carry_ref):
    i = pl.program_id(0)

    @pl.when(i == 0)
    def _():
        carry_ref[...] = jnp.zeros_like(carry_ref)

    e = e_ref[...]
    ts = e.shape[1]
    onehot = lax.broadcasted_iota(I32, (CLASS_ROWS, ts), 0) == e
    cum = jnp.dot(jnp.where(onehot, 1.0, 0.0).astype(BF16), tri_ref[...],
                  preferred_element_type=F32)
    carry = carry_ref[...]
    total = cum + carry[:, 0:1]
    rank_ref[...] = (jnp.sum(jnp.where(onehot, total, 0.0), axis=0, keepdims=True) - 1.0).astype(I32)
    new_carry = carry + cum[:, ts - 1:ts]
    carry_ref[...] = new_carry
    cnt_ref[...] = new_carry.astype(I32)


def _ranks(cls):
    n_slots = cls.shape[1]
    ts = TS_RANK
    tri = (jnp.arange(ts)[:, None] <= jnp.arange(ts)[None, :]).astype(BF16)
    return pl.pallas_call(
        _rank_kernel,
        grid=(n_slots // ts,),
        in_specs=[pl.BlockSpec((1, ts), lambda i: (0, i)),
                  pl.BlockSpec((ts, ts), lambda i: (0, 0))],
        out_specs=[pl.BlockSpec((1, ts), lambda i: (0, i)),
                   pl.BlockSpec((CLASS_ROWS, LANES), lambda i: (0, 0))],
        out_shape=[jax.ShapeDtypeStruct((1, n_slots), I32),
                   jax.ShapeDtypeStruct((CLASS_ROWS, LANES), I32)],
        scratch_shapes=[pltpu.VMEM((CLASS_ROWS, LANES), F32)],
        compiler_params=_params("arbitrary"),
        name="slot_ranks",
    )(cls, tri)


def _pos_kernel(off_ref, e_ref, rank_ref, pos_ref):
    e = e_ref[...]
    pos = rank_ref[...]
    for k in range(N_CLASSES):
        pos = pos + jnp.where(e == k, off_ref[k], 0)
    pos_ref[...] = pos


def _positions(off, e_flat, rank):
    n_slots = rank.shape[1]
    ts = min(n_slots, 8192)
    spec = pl.BlockSpec((1, ts), lambda i, *_: (0, i))
    return pl.pallas_call(
        _pos_kernel,
        grid_spec=pltpu.PrefetchScalarGridSpec(
            num_scalar_prefetch=1, grid=(n_slots // ts,), in_specs=[spec, spec], out_specs=spec),
        out_shape=jax.ShapeDtypeStruct((1, n_slots), I32),
        compiler_params=_params("arbitrary"),
        name="slot_positions",
    )(off, e_flat, rank)


def _row_copy(src_ref, r, dst_ref, p, sem):
    return pltpu.make_async_copy(src_ref.at[pl.ds(r, 1)], dst_ref.at[pl.ds(p, 1)], sem)


def _rows_wait(src_ref, dst_ref, sem):
    pltpu.make_async_copy(src_ref, dst_ref, sem).wait()


def _dispatch_kernel(end_ref, cnt_ref, pos_ref, hp_ref, xs_ref, zero_ref, sem, zsem):
    i = pl.program_id(0)
    tm = hp_ref.shape[0]
    tz = zero_ref.shape[0]

    @pl.when(i == 0)
    def _():
        zero_ref[...] = jnp.zeros_like(zero_ref)
        n_rows = xs_ref.shape[0]
        used = end_ref[N_CLASSES - 1]

        def clear(start):
            return pltpu.make_async_copy(zero_ref, xs_ref.at[pl.ds(pl.multiple_of(start, tz), tz)], zsem)

        for wait in (False, True):
            for e in range(N_CLASSES):
                @pl.when(cnt_ref[e] > 0)
                def _():
                    cp = clear(end_ref[e] - tz)
                    cp.wait() if wait else cp.start()

                @pl.when(used + e * tz < n_rows)
                def _():
                    cp = clear(used + e * tz)
                    cp.wait() if wait else cp.start()

    def issue(h, _):
        for k in range(2):
            r = 2 * h + k
            _row_copy(hp_ref, r, xs_ref, pos_ref[0, 0, r], sem).start(priority=k)
        return 0

    lax.fori_loop(0, tm // 2, issue, 0, unroll=8)
    _rows_wait(hp_ref, xs_ref.at[pl.ds(0, tm)], sem)


def _dispatch(end, cnt, pos3, hp, n_rows):
    t, w = hp.shape
    tm = TM_DISP
    return pl.pallas_call(
        _dispatch_kernel,
        grid_spec=pltpu.PrefetchScalarGridSpec(
            num_scalar_prefetch=2,
            grid=(t // tm,),
            in_specs=[pl.BlockSpec((1, 1, tm), lambda i, *_: (i, 0, 0), memory_space=pltpu.SMEM),
                      pl.BlockSpec((tm, w), lambda i, *_: (i, 0))],
            out_specs=pl.BlockSpec(memory_space=pl.ANY),
            scratch_shapes=[pltpu.VMEM((TM_EXP, w), U32), pltpu.SemaphoreType.DMA(()),
                            pltpu.SemaphoreType.DMA(())]),
        out_shape=jax.ShapeDtypeStruct((n_rows, w), U32),
        compiler_params=_params("arbitrary"),
        name="dispatch",
    )(end, cnt, pos3, hp)


def _expert_kernel(ta_ref, tb_ref, nv_ref, xs_ref, wga_ref, wua_ref, wda_ref, wgb_ref, wub_ref, wdb_ref,
                   ys_ref):
    i = pl.program_id(0)
    hw = ys_ref.shape[1]

    @pl.when(i < nv_ref[0])
    def _():
        x = _load_packed_rows(xs_ref, hw).astype(BF16)
        gates = lax.bitcast_convert_type(xs_ref[:, hw:], F32)

        def mlp(wg_ref, wu_ref, wd_ref):
            g = jnp.dot(x, wg_ref[0], preferred_element_type=F32)
            u = jnp.dot(x, wu_ref[0], preferred_element_type=F32)
            he = (_silu(g) * u).astype(BF16)
            return jnp.dot(he, wd_ref[0], preferred_element_type=F32)

        y = gates[:, 0:1] * mlp(wga_ref, wua_ref, wda_ref) + gates[:, 1:2] * mlp(wgb_ref, wub_ref, wdb_ref)
        _store_packed_rows(ys_ref, y)

    @pl.when(i >= nv_ref[0])
    def _():
        ys_ref[...] = jnp.zeros_like(ys_ref)


def _experts(tile_ea, tile_eb, n_valid, xs, wg_bf, wu_bf, wd_bf):
    n_rows, w = xs.shape
    tm = TM_EXP
    _, d, de = wg_bf.shape

    def row_map(i, ta, tb, nv):
        return (jnp.minimum(i, nv[0] - 1), 0)

    def w_spec(shape, first):
        if first:
            return pl.BlockSpec(shape, lambda i, ta, tb, nv: (ta[i], 0, 0))
        return pl.BlockSpec(shape, lambda i, ta, tb, nv: (tb[i], 0, 0))

    return pl.pallas_call(
        _expert_kernel,
        grid_spec=pltpu.PrefetchScalarGridSpec(
            num_scalar_prefetch=3,
            grid=(n_rows // tm,),
            in_specs=[pl.BlockSpec((tm, w), row_map),
                      w_spec((1, d, de), True), w_spec((1, d, de), True), w_spec((1, de, d), True),
                      w_spec((1, d, de), False), w_spec((1, d, de), False), w_spec((1, de, d), False)],
            out_specs=pl.BlockSpec((tm, d // 2), lambda i, ta, tb, nv: (i, 0))),
        out_shape=jax.ShapeDtypeStruct((n_rows, d // 2), U32),
        compiler_params=_params("arbitrary"),
        name="experts",
    )(tile_ea, tile_eb, n_valid, xs, wg_bf, wu_bf, wd_bf, wg_bf, wu_bf, wd_bf)


def _combine_kernel(pos_ref, pos_next_ref, ys_ref, x1_ref, g2_ref, lng_ref, lnb_ref,
                    o_ref, buf_ref, sems, *, alpha):
    i = pl.program_id(0)
    tm = x1_ref.shape[0]
    slot = lax.rem(i, 2)

    def gather(p_ref, s):
        def issue(h, _):
            for k in range(2):
                r = 2 * h + k
                _row_copy(ys_ref, p_ref[0, 0, r], buf_ref.at[s], r, sems.at[s]).start(priority=k)
            return 0

        lax.fori_loop(0, tm // 2, issue, 0, unroll=8)

    @pl.when(i == 0)
    def _():
        gather(pos_ref, 0)

    @pl.when(i + 1 < pl.num_programs(0))
    def _():
        gather(pos_next_ref, 1 - slot)

    _rows_wait(ys_ref.at[pl.ds(0, tm)], buf_ref.at[slot], sems.at[slot])
    y = _load_packed_rows(buf_ref.at[slot], buf_ref.shape[2])
    z = alpha * x1_ref[...] + g2_ref[0] * y
    o_ref[...] = _layer_norm(z) * lng_ref[...] + lnb_ref[...]


def _combine(pos3, ys, x1, g2, ln_g, ln_b, seq, alpha):
    t, d = x1.shape
    w = ys.shape[1]
    tm = TM_DISP
    nb = t // tm
    per_b = seq // tm
    row_spec = pl.BlockSpec((1, d), lambda i: (0, 0))
    return pl.pallas_call(
        functools.partial(_combine_kernel, alpha=alpha),
        grid=(nb,),
        in_specs=[pl.BlockSpec((1, 1, tm), lambda i: (i, 0, 0), memory_space=pltpu.SMEM),
                  pl.BlockSpec((1, 1, tm), lambda i: (jnp.minimum(i + 1, nb - 1), 0, 0),
                               memory_space=pltpu.SMEM),
                  pl.BlockSpec(memory_space=pl.ANY),
                  pl.BlockSpec((tm, d), lambda i: (i, 0)),
                  pl.BlockSpec((1, 1, d), lambda i: (i // per_b, 0, 0)),
                  row_spec, row_spec],
        out_specs=pl.BlockSpec((tm, d), lambda i: (i, 0)),
        out_shape=jax.ShapeDtypeStruct((t, d), F32),
        scratch_shapes=[pltpu.VMEM((2, tm, w), U32), pltpu.SemaphoreType.DMA((2,))],
        compiler_params=_params("arbitrary"),
        name="combine",
    )(pos3, pos3, ys, x1, g2, ln_g.reshape(1, d), ln_b.reshape(1, d))


def _class_experts():
    lo, hi = [], []
    for g in range(N_GROUPS):
        for i, j in GROUP_PAIRS:
            lo.append(g * EXPERTS_PER_GROUP + i)
            hi.append(g * EXPERTS_PER_GROUP + j)
    return jnp.array(lo, I32), jnp.array(hi, I32)


def _moe(hp, cls, x1, g2, ln_g, ln_b, wg_bf, wu_bf, wd_bf, seq, alpha):
    t = hp.shape[0]
    n_tiles = t // TM_EXP + N_CLASSES
    rank, cnt = _ranks(cls)
    counts = cnt[:N_CLASSES, 0]
    tiles_c = (counts + (TM_EXP - 1)) // TM_EXP
    tile_end = jnp.cumsum(tiles_c)
    end = (tile_end * TM_EXP).astype(I32)
    off = end - (tiles_c * TM_EXP).astype(I32)
    n_valid = tile_end[-1:].astype(I32)
    tile_ids = jnp.arange(n_tiles, dtype=I32)
    tile_cls = jnp.sum(tile_ids[:, None] >= tile_end[None, :], axis=1).astype(I32)
    tile_cls = jnp.minimum(tile_cls, tile_cls[jnp.maximum(n_valid[0] - 1, 0)])
    cls_lo, cls_hi = _class_experts()
    nb = t // TM_DISP
    pos3 = _positions(off, cls, rank).reshape(nb, 1, TM_DISP)
    xs = _dispatch(end, counts, pos3, hp, n_tiles * TM_EXP)
    ys = _experts(cls_lo[tile_cls], cls_hi[tile_cls], n_valid, xs, wg_bf, wu_bf, wd_bf)
    return _combine(pos3, ys, x1, g2, ln_g, ln_b, seq, alpha)


def _rotary_tables(seq):
    half = RET_QK // 2
    inv = 1.0 / (ROPE_BASE ** (jnp.arange(0, RET_QK, 2, dtype=F32) / RET_QK))
    ang = jnp.arange(seq, dtype=F32)[:, None] * inv[None, :]
    cos = jnp.cos(ang)
    sin = jnp.sin(ang)
    reps = LANES // RET_QK
    cos_t = jnp.tile(jnp.concatenate([cos, cos], axis=1), (1, reps))
    sin_t = jnp.tile(jnp.concatenate([-sin, sin], axis=1), (1, reps))
    del half
    return cos_t, sin_t


def kernel(x, c, w_ada, b_ada, w_in, w_out, lambda_q1, lambda_k1, lambda_q2, lambda_k2, diff_subln,
           ln_mix_g, ln_mix_b, ln_ffn_g, ln_ffn_b, router_w, router_b, w_gate, w_up, w_down):
    batch, seq, d = x.shape
    depth = w_ada.shape[0]
    alpha = (2 * depth) ** 0.25
    mod = _adaln(c, w_ada, b_ada)
    cos_t, sin_t = _rotary_tables(seq)
    rw_t = router_w.T.astype(BF16)
    xf = x.reshape(batch * seq, d)
    for l in range(depth):
        m = mod[l].reshape(batch, N_MOD, 1, d)
        sh1, sc1, g1, sh2, sc2, g2 = (m[:, i] for i in range(N_MOD))
        rq, rk, rv, rg, dq, dk, dvt = _inproj(xf, sc1, sh1, w_in[l].astype(BF16), cos_t, sin_t, seq)
        ret = _retention(rq, rk, rv, rg, batch, seq)
        lambda_init = 0.8 - 0.6 * math.exp(-0.3 * l)
        dif = _diff_attention(dq, dk, dvt, lambda_q1[l], lambda_k1[l], lambda_q2[l], lambda_k2[l],
                              diff_subln[l], lambda_init, batch, seq)
        x1, hp, cls = _outproj(ret, dif, xf, g1, sc2, sh2, w_out[l].astype(BF16),
                               ln_mix_g[l], ln_mix_b[l], rw_t, router_b, seq, alpha)
        xf = _moe(hp, cls, x1, g2, ln_ffn_g[l], ln_ffn_b[l], w_gate[l].astype(BF16),
                  w_up[l].astype(BF16), w_down[l].astype(BF16), seq, alpha)
    return xf.reshape(batch, seq, d)
```

```python
import functools
import math

import jax
import jax.numpy as jnp
from jax import lax
from jax.experimental import pallas as pl
from jax.experimental.pallas import tpu as pltpu

F32 = jnp.float32
BF16 = jnp.bfloat16
U32 = jnp.uint32
I32 = jnp.int32

RET_HEADS = 8
RET_QK = 64
DIFF_HEADS = 4
DIFF_QK = 64
HEAD_PAIR = 128
SEG = 512
N_SEG = 7
CHUNK = 128
ROPE_BASE = 10000.0
SUBLN_EPS = 1e-5
LN_EPS = 1e-5
N_EXPERTS = 16
N_GROUPS = 4
EXPERTS_PER_GROUP = 4
TOP_K = 2
GROUP_PAIRS = tuple((i, j) for i in range(EXPERTS_PER_GROUP) for j in range(i + 1, EXPERTS_PER_GROUP))
N_CLASSES = N_GROUPS * len(GROUP_PAIRS)
CLASS_ROWS = 32
N_MOD = 6
LANES = 128
BF16_SUBLANES = 16
VMEM_LIMIT = 56 * 1024 * 1024

TM_PROJ = 1024
RET_GROUP = 8
CT = 256
BK = 512
TS_RANK = 512
TM_DISP = 512
TM_EXP = 512


def _params(*sem):
    return pltpu.CompilerParams(dimension_semantics=sem, vmem_limit_bytes=VMEM_LIMIT)


def _layer_norm(x):
    mu = jnp.mean(x, axis=-1, keepdims=True)
    xc = x - mu
    var = jnp.mean(xc * xc, axis=-1, keepdims=True)
    return xc * lax.rsqrt(var + LN_EPS)


def _silu(x):
    return x * jax.nn.sigmoid(x)


def _adaln_kernel(c_ref, w_ref, b_ref, o_ref):
    cond = _silu(c_ref[...]).astype(BF16)
    o_ref[0] = jnp.dot(cond, w_ref[0].astype(BF16), preferred_element_type=F32) + b_ref[0]


def _adaln(c, w_ada, b_ada):
    depth, d, n = w_ada.shape
    b = c.shape[0]
    tn = 1536
    return pl.pallas_call(
        _adaln_kernel,
        grid=(depth, n // tn),
        in_specs=[pl.BlockSpec((b, d), lambda l, j: (0, 0)),
                  pl.BlockSpec((1, d, tn), lambda l, j: (l, 0, j)),
                  pl.BlockSpec((1, 1, tn), lambda l, j: (l, 0, j))],
        out_specs=pl.BlockSpec((1, b, tn), lambda l, j: (l, 0, j)),
        out_shape=jax.ShapeDtypeStruct((depth, b, n), F32),
        compiler_params=_params("arbitrary", "arbitrary"),
        name="adaln",
    )(c, w_ada, b_ada.reshape(depth, 1, n))


def _inproj_kernel(x_ref, sc_ref, sh_ref, w_ref, wvt_ref, cos_ref, sin_ref, *out_refs):
    h = (_layer_norm(x_ref[...]) * (1.0 + sc_ref[0]) + sh_ref[0]).astype(BF16)
    cos = cos_ref[...]
    sin = sin_ref[...]
    lane = lax.broadcasted_iota(I32, cos.shape, 1)
    first_half = (lane & (RET_QK - 1)) < (RET_QK // 2)

    def rotary(p):
        outs = []
        for c in range(SEG // LANES):
            pc = p[:, c * LANES:(c + 1) * LANES]
            swapped = jnp.where(first_half, pltpu.roll(pc, LANES - RET_QK // 2, 1),
                                pltpu.roll(pc, RET_QK // 2, 1))
            outs.append(pc * cos + swapped * sin)
        return jnp.concatenate(outs, axis=1)

    for s, o_ref in enumerate(out_refs[:-1]):
        p = jnp.dot(h, w_ref[:, s * SEG:(s + 1) * SEG], preferred_element_type=F32)
        if s in (0, 1):
            p = rotary(p)
        if s == 0:
            p = p * (RET_QK ** -0.5)
        if s == 4:
            p = p * (DIFF_QK ** -0.5 * math.log2(math.e))
        o_ref[...] = p.astype(BF16)
    out_refs[-1][...] = lax.dot_general(wvt_ref[...], h, (((1,), (1,)), ((), ())),
                                        preferred_element_type=F32).astype(BF16)


def _inproj(x2d, sc, sh, w_all, w_vt_all, layer, cos_t, sin_t, seq):
    t, d = x2d.shape
    tm = TM_PROJ
    per_b = seq // tm
    n_main = (N_SEG - 1) * SEG
    return pl.pallas_call(
        _inproj_kernel,
        grid=(t // tm,),
        in_specs=[pl.BlockSpec((tm, d), lambda i: (i, 0)),
                  pl.BlockSpec((1, 1, d), lambda i: (i // per_b, 0, 0)),
                  pl.BlockSpec((1, 1, d), lambda i: (i // per_b, 0, 0)),
                  pl.BlockSpec((None, d, n_main), lambda i: (layer, 0, 0)),
                  pl.BlockSpec((None, SEG, d), lambda i: (layer, 0, 0)),
                  pl.BlockSpec((tm, LANES), lambda i: (i % per_b, 0)),
                  pl.BlockSpec((tm, LANES), lambda i: (i % per_b, 0))],
        out_specs=[pl.BlockSpec((tm, SEG), lambda i: (i, 0))] * (N_SEG - 1)
        + [pl.BlockSpec((SEG, tm), lambda i: (0, i))],
        out_shape=[jax.ShapeDtypeStruct((t, SEG), BF16)] * (N_SEG - 1)
        + [jax.ShapeDtypeStruct((SEG, t), BF16)],
        compiler_params=_params("arbitrary"),
        name="inproj",
    )(x2d, sc, sh, w_all, w_vt_all, cos_t, sin_t)


def _ret_kernel(q_ref, k_ref, v_ref, g_ref, d2_ref, qd_ref, kd_ref, gm_ref, o_ref, *, n_chunks):
    c = CHUNK
    lo = lax.broadcasted_iota(I32, (c, HEAD_PAIR), 1) < RET_QK
    r = lax.broadcasted_iota(I32, (HEAD_PAIR, HEAD_PAIR), 0) < RET_QK
    cc = lax.broadcasted_iota(I32, (HEAD_PAIR, HEAD_PAIR), 1) < RET_QK
    same_head = r == cc
    d2 = d2_ref[0]
    qd = qd_ref[0]
    kd = kd_ref[0]
    gm = gm_ref[0]

    def split_heads(a):
        zero = jnp.zeros_like(a)
        return jnp.concatenate([jnp.where(lo, a, zero), jnp.where(lo, zero, a)], axis=0)

    grp = RET_GROUP

    def body(t, state):
        sls = [pl.ds(pl.multiple_of((t * grp + g) * c, c), c) for g in range(grp)]
        qs = [q_ref[sl, :] for sl in sls]
        ks = [k_ref[sl, :] for sl in sls]
        vs = [v_ref[sl, :] for sl in sls]
        s2s = [lax.dot_general(split_heads(q), k, (((1,), (1,)), ((), ())), preferred_element_type=F32)
               for q, k in zip(qs, ks)]
        kvs = [lax.dot_general((k.astype(F32) * kd).astype(BF16), v, (((0,), (0,)), ((), ())),
                               preferred_element_type=F32) for k, v in zip(ks, vs)]
        states = []
        for kv in kvs:
            states.append(state)
            state = state * gm + jnp.where(same_head, kv, 0.0)
        outs = []
        for q, v, s2, st in zip(qs, vs, s2s, states):
            p2 = (s2 * d2).astype(BF16)
            pcat = jnp.concatenate([p2[:c], p2[c:]], axis=1)
            inner = jnp.dot(pcat, split_heads(v), preferred_element_type=F32)
            qdq = (q.astype(F32) * qd).astype(BF16)
            outs.append(inner + jnp.dot(qdq, st.astype(BF16), preferred_element_type=F32))
        inv = 1.0 / RET_QK
        for sl, o in zip(sls, outs):
            s_all = jnp.sum(o, axis=-1, keepdims=True)
            s_lo = jnp.sum(jnp.where(lo, o, 0.0), axis=-1, keepdims=True)
            dlt = o - jnp.where(lo, s_lo, s_all - s_lo) * inv
            dd = dlt * dlt
            v_all = jnp.sum(dd, axis=-1, keepdims=True)
            v_lo = jnp.sum(jnp.where(lo, dd, 0.0), axis=-1, keepdims=True)
            var = jnp.where(lo, v_lo, v_all - v_lo) * inv
            y = dlt * lax.rsqrt(var + LN_EPS)
            o_ref[sl, :] = (y * _silu(g_ref[sl, :].astype(F32))).astype(BF16)
        return state

    lax.fori_loop(0, n_chunks // grp, body, jnp.zeros((HEAD_PAIR, HEAD_PAIR), F32))


def _retention_tables():
    c = CHUNK
    log_g = jnp.log(1.0 - 2.0 ** (-5.0 - jnp.arange(RET_HEADS, dtype=F32)))
    idx = jnp.arange(c, dtype=F32)
    rel = idx[:, None] - idx[None, :]
    decay = jnp.where(rel >= 0, jnp.exp(log_g[:, None, None] * jnp.maximum(rel, 0.0)), 0.0)
    d2 = decay.reshape(RET_HEADS // 2, 2 * c, c)
    lane_head = jnp.arange(HEAD_PAIR) // RET_QK
    pair_log = log_g.reshape(RET_HEADS // 2, 2)[:, lane_head]
    qd = jnp.exp(pair_log[:, None, :] * (idx + 1.0)[None, :, None])
    kd = jnp.exp(pair_log[:, None, :] * (c - 1.0 - idx)[None, :, None])
    gm = jnp.broadcast_to(jnp.exp(pair_log * c)[:, :, None], (RET_HEADS // 2, HEAD_PAIR, HEAD_PAIR))
    return d2, qd, kd, gm


def _retention(rq, rk, rv, rg, batch, seq):
    t = rq.shape[0]
    n_pairs = RET_HEADS // 2
    d2, qd, kd, gm = _retention_tables()
    seq_spec = pl.BlockSpec((seq, HEAD_PAIR), lambda b, p: (b, p))
    return pl.pallas_call(
        functools.partial(_ret_kernel, n_chunks=seq // CHUNK),
        grid=(batch, n_pairs),
        in_specs=[seq_spec, seq_spec, seq_spec, seq_spec,
                  pl.BlockSpec((1, 2 * CHUNK, CHUNK), lambda b, p: (p, 0, 0)),
                  pl.BlockSpec((1, CHUNK, HEAD_PAIR), lambda b, p: (p, 0, 0)),
                  pl.BlockSpec((1, CHUNK, HEAD_PAIR), lambda b, p: (p, 0, 0)),
                  pl.BlockSpec((1, HEAD_PAIR, HEAD_PAIR), lambda b, p: (p, 0, 0))],
        out_specs=seq_spec,
        out_shape=jax.ShapeDtypeStruct((t, SEG), BF16),
        compiler_params=_params("arbitrary", "arbitrary"),
        name="retention",
    )(rq, rk, rv, rg, d2, qd, kd, gm)


def _diff_kernel(lq1_ref, lk1_ref, lq2_ref, lk2_ref, sg_ref, q_ref, k_ref, vt_ref, o_ref,
                 vta_ref, sa_ref, sb_ref, mx_ref, m_ref, acc_ref, *, lambda_init):
    bq, bk, ct = q_ref.shape[0], BK, CT
    n_ct = 2 * bq // ct
    s_refs = (sa_ref, sb_ref)

    vta_ref[:HEAD_PAIR, :] = vt_ref[...]
    rid = lax.broadcasted_iota(I32, (vta_ref.shape[0] - HEAD_PAIR, vta_ref.shape[1]), 0)
    vta_ref[HEAD_PAIR:, :] = jnp.where(rid == 0, 1.0, 0.0).astype(BF16)

    lam = (jnp.exp(jnp.sum(lq1_ref[...] * lk1_ref[...], axis=-1, keepdims=True))
           - jnp.exp(jnp.sum(lq2_ref[...] * lk2_ref[...], axis=-1, keepdims=True)) + lambda_init)
    q = q_ref[...]
    lo = lax.broadcasted_iota(I32, q.shape, 1) < DIFF_QK
    zero = jnp.zeros_like(q)
    q2 = jnp.concatenate([jnp.where(lo, q, zero), jnp.where(lo, zero, q)], axis=0)
    q_tiles = [q2[c * ct:(c + 1) * ct, :] for c in range(n_ct)]

    def block_modes(j):
        modes = []
        for c in range(n_ct):
            q_lo = (c * ct) % bq
            if j * bk >= q_lo + ct:
                modes.append("skip")
            elif (j + 1) * bk - 1 <= q_lo:
                modes.append("full")
            else:
                modes.append("masked")
        return tuple(modes)

    def scores(j, slot, modes):
        k_blk = k_ref[j * bk:(j + 1) * bk, :]
        for c, qt in enumerate(q_tiles):
            if modes[c] == "skip":
                continue
            cols = slice(c * ct, (c + 1) * ct)
            s = lax.dot_general(k_blk, qt, (((1,), (1,)), ((), ())), preferred_element_type=F32)
            if modes[c] == "masked":
                kpos = j * bk + lax.broadcasted_iota(I32, s.shape, 0)
                qpos = (c * ct) % bq + lax.broadcasted_iota(I32, s.shape, 1)
                s = jnp.where(kpos <= qpos, s, -jnp.inf)
            s_refs[slot][:, cols] = s
            mx_ref[slot:slot + 1, cols] = jnp.max(s, axis=0, keepdims=True)

    def update(j, slot, modes):
        vta_blk = vta_ref[:, j * bk:(j + 1) * bk]
        for c in range(n_ct):
            if modes[c] == "skip":
                continue
            cols = slice(c * ct, (c + 1) * ct)
            m = m_ref[:, cols]
            m_new = jnp.maximum(m, mx_ref[slot:slot + 1, cols])
            p = jnp.exp2(s_refs[slot][:, cols] - m_new).astype(BF16)
            acc_ref[:, cols] = (jnp.exp2(m - m_new) * acc_ref[:, cols]
                                + jnp.dot(vta_blk, p, preferred_element_type=F32))
            m_ref[:, cols] = m_new

    m_ref[...] = jnp.full(m_ref.shape, -jnp.inf, F32)
    acc_ref[...] = jnp.zeros(acc_ref.shape, F32)
    n_blocks = bq // bk
    scores(0, 0, block_modes(0))
    for j in range(n_blocks):
        if j + 1 < n_blocks:
            scores(j + 1, (j + 1) % 2, block_modes(j + 1))
        update(j, j % 2, block_modes(j))
    acc = acc_ref[...]
    a = acc[:HEAD_PAIR] / acc[HEAD_PAIR:HEAD_PAIR + 1]
    out = (a[:, :bq] - lam * a[:, bq:]).T
    ms = jnp.mean(out * out, axis=-1, keepdims=True)
    out = out * lax.rsqrt(ms + SUBLN_EPS) * sg_ref[...] * (1.0 - lambda_init)
    o_ref[...] = out.astype(BF16)


def _diff_attention(dq, dk, dvt, lq1, lk1, lq2, lk2, subln, lambda_init, batch, seq):
    t = dq.shape[0]
    lam_spec = pl.BlockSpec((1, DIFF_QK), lambda b, h: (0, 0))
    seq_spec = pl.BlockSpec((seq, HEAD_PAIR), lambda b, h: (b, h))
    n_rows = HEAD_PAIR + BF16_SUBLANES
    return pl.pallas_call(
        functools.partial(_diff_kernel, lambda_init=lambda_init),
        grid=(batch, DIFF_HEADS),
        in_specs=[lam_spec, lam_spec, lam_spec, lam_spec,
                  pl.BlockSpec((1, HEAD_PAIR), lambda b, h: (0, 0)),
                  seq_spec, seq_spec,
                  pl.BlockSpec((HEAD_PAIR, seq), lambda b, h: (h, b))],
        out_specs=seq_spec,
        out_shape=jax.ShapeDtypeStruct((t, SEG), BF16),
        scratch_shapes=[pltpu.VMEM((n_rows, seq), BF16),
                        pltpu.VMEM((BK, 2 * seq), F32), pltpu.VMEM((BK, 2 * seq), F32),
                        pltpu.VMEM((2, 2 * seq), F32), pltpu.VMEM((1, 2 * seq), F32),
                        pltpu.VMEM((n_rows, 2 * seq), F32)],
        compiler_params=_params("arbitrary", "arbitrary"),
        name="diff_attention",
    )(lq1.reshape(1, -1), lk1.reshape(1, -1), lq2.reshape(1, -1), lk2.reshape(1, -1),
      subln.reshape(1, -1), dq, dk, dvt)


def _pack_scratch(rows, hw):
    return pltpu.VMEM((hw // LANES, 2 * rows, LANES), F32)


def _packed_zeros(shape):
    rows, hw = shape
    return pltpu.bitcast(jnp.zeros((2 * rows, hw), BF16), U32)


def _store_packed_rows(ref, scr, lo, hi):
    rows, hw = ref.shape
    for c in range(hw // LANES):
        cols = slice(c * LANES, (c + 1) * LANES)
        scr[c, pl.ds(0, rows, stride=2), :] = lo[:, cols]
        scr[c, pl.ds(1, rows, stride=2), :] = hi[:, cols]
    for c in range(hw // LANES):
        ref[:, c * LANES:(c + 1) * LANES] = pltpu.bitcast(scr[c].astype(BF16), U32)


def _load_packed_rows(ref, scr):
    rows, hw = ref.shape
    for c in range(hw // LANES):
        scr[c] = pltpu.bitcast(ref[:, c * LANES:(c + 1) * LANES], BF16).astype(F32)
    lo = jnp.concatenate([scr[c, pl.ds(0, rows, stride=2), :] for c in range(hw // LANES)], axis=1)
    hi = jnp.concatenate([scr[c, pl.ds(1, rows, stride=2), :] for c in range(hw // LANES)], axis=1)
    return lo, hi


def _first_top2(vals):
    m1 = jnp.maximum(jnp.maximum(vals[0], vals[1]), jnp.maximum(vals[2], vals[3]))
    i1 = jnp.where(vals[0] == m1, 0, jnp.where(vals[1] == m1, 1, jnp.where(vals[2] == m1, 2, 3)))
    rest = [jnp.where(i1 == i, -1.0, v) for i, v in enumerate(vals)]
    m2 = jnp.maximum(jnp.maximum(rest[0], rest[1]), jnp.maximum(rest[2], rest[3]))
    i2 = jnp.where(rest[0] == m2, 0, jnp.where(rest[1] == m2, 1, jnp.where(rest[2] == m2, 2, 3)))
    return m1, i1, m2, i2


def _outproj_kernel(ret_ref, dif_ref, x_ref, g1_ref, sc2_ref, sh2_ref, wo_ref, lng_ref, lnb_ref,
                    rw_ref, rb_ref, x1_ref, hp_ref, cls_ref, pack_ref, *, alpha):
    half = ret_ref.shape[1]
    hw = x_ref.shape[1] // 2
    mixed = (jnp.dot(ret_ref[...], wo_ref[:half, :], preferred_element_type=F32)
             + jnp.dot(dif_ref[...], wo_ref[half:, :], preferred_element_type=F32))
    x1 = _layer_norm(alpha * x_ref[...] + g1_ref[0] * mixed) * lng_ref[...] + lnb_ref[...]
    x1_ref[...] = x1
    h2 = _layer_norm(x1) * (1.0 + sc2_ref[0]) + sh2_ref[0]
    logits = lax.dot_general(rw_ref[...], h2.astype(BF16), (((1,), (1,)), ((), ())),
                             preferred_element_type=F32) + rb_ref[...]
    e = jnp.exp(logits - jnp.max(logits, axis=0, keepdims=True))
    probs = e / jnp.sum(e, axis=0, keepdims=True)
    rows = [probs[i:i + 1, :] for i in range(N_EXPERTS)]
    group_tops = []
    for g in range(N_GROUPS):
        m1, _, m2, _ = _first_top2(rows[g * EXPERTS_PER_GROUP:(g + 1) * EXPERTS_PER_GROUP])
        group_tops.append(m1 + m2)
    gmax = jnp.maximum(jnp.maximum(group_tops[0], group_tops[1]), jnp.maximum(group_tops[2], group_tops[3]))
    gbest = jnp.where(group_tops[0] == gmax, 0,
                      jnp.where(group_tops[1] == gmax, 1, jnp.where(group_tops[2] == gmax, 2, 3)))
    sel = []
    for i in range(EXPERTS_PER_GROUP):
        v = rows[3 * EXPERTS_PER_GROUP + i]
        for g in (2, 1, 0):
            v = jnp.where(gbest == g, rows[g * EXPERTS_PER_GROUP + i], v)
        sel.append(v)
    p0, i0, p1, i1 = _first_top2(sel)
    denom = p0 + p1
    first_is_low = i0 < i1
    i_lo = jnp.where(first_is_low, i0, i1)
    i_hi = jnp.where(first_is_low, i1, i0)
    pair_base = jnp.where(i_lo == 0, 0, jnp.where(i_lo == 1, 3, 5))
    cls_ref[...] = gbest * len(GROUP_PAIRS) + pair_base + (i_hi - i_lo - 1)
    g_lo = jnp.where(first_is_low, p0, p1) / denom
    g_hi = jnp.where(first_is_low, p1, p0) / denom
    tm = x1.shape[0]
    rid = lax.broadcasted_iota(I32, (LANES, tm), 0)
    gcols = jnp.where(rid == 0, g_lo, jnp.where(rid == 1, g_hi, 0.0)).T
    _store_packed_rows(hp_ref, pack_ref, jnp.concatenate([h2[:, :hw], gcols], axis=1),
                       jnp.concatenate([h2[:, hw:], gcols], axis=1))


def _outproj(ret, dif, x2d, g1, sc2, sh2, wo_all, layer, ln_g, ln_b, rw_t, rb, seq, alpha):
    t, d = x2d.shape
    tm = TM_PROJ
    per_b = seq // tm
    mod_spec = pl.BlockSpec((1, 1, d), lambda i: (i // per_b, 0, 0))
    row_spec = pl.BlockSpec((1, d), lambda i: (0, 0))
    return pl.pallas_call(
        functools.partial(_outproj_kernel, alpha=alpha),
        grid=(t // tm,),
        in_specs=[pl.BlockSpec((tm, SEG), lambda i: (i, 0)),
                  pl.BlockSpec((tm, SEG), lambda i: (i, 0)),
                  pl.BlockSpec((tm, d), lambda i: (i, 0)),
                  mod_spec, mod_spec, mod_spec,
                  pl.BlockSpec((None, 2 * SEG, d), lambda i: (layer, 0, 0)),
                  row_spec, row_spec,
                  pl.BlockSpec((N_EXPERTS, d), lambda i: (0, 0)),
                  pl.BlockSpec((N_EXPERTS, 1), lambda i: (0, 0))],
        out_specs=[pl.BlockSpec((tm, d), lambda i: (i, 0)),
                   pl.BlockSpec((tm, d // 2 + LANES), lambda i: (i, 0)),
                   pl.BlockSpec((1, tm), lambda i: (0, i))],
        out_shape=[jax.ShapeDtypeStruct((t, d), F32),
                   jax.ShapeDtypeStruct((t, d // 2 + LANES), U32),
                   jax.ShapeDtypeStruct((1, t), I32)],
        scratch_shapes=[_pack_scratch(tm, d // 2 + LANES)],
        compiler_params=_params("arbitrary"),
        name="outproj_router",
    )(ret, dif, x2d, g1, sc2, sh2, wo_all, ln_g.reshape(1, d), ln_b.reshape(1, d), rw_t, rb.reshape(-1, 1))


def _rank_kernel(e_ref, tri_ref, rank_ref, cnt_ref, carry_ref):
    i = pl.program_id(0)

    @pl.when(i == 0)
    def _():
        carry_ref[...] = jnp.zeros_like(carry_ref)

    e = e_ref[...]
    ts = e.shape[1]
    onehot = lax.broadcasted_iota(I32, (CLASS_ROWS, ts), 0) == e
    cum = jnp.dot(jnp.where(onehot, 1.0, 0.0).astype(BF16), tri_ref[...],
                  preferred_element_type=F32)
    carry = carry_ref[...]
    total = cum + carry[:, 0:1]
    rank_ref[...] = (jnp.sum(jnp.where(onehot, total, 0.0), axis=0, keepdims=True) - 1.0).astype(I32)
    new_carry = carry + cum[:, ts - 1:ts]
    carry_ref[...] = new_carry
    cnt_ref[...] = new_carry.astype(I32)


def _ranks(cls):
    n_slots = cls.shape[1]
    ts = TS_RANK
    tri = (jnp.arange(ts)[:, None] <= jnp.arange(ts)[None, :]).astype(BF16)
    return pl.pallas_call(
        _rank_kernel,
        grid=(n_slots // ts,),
        in_specs=[pl.BlockSpec((1, ts), lambda i: (0, i)),
                  pl.BlockSpec((ts, ts), lambda i: (0, 0))],
        out_specs=[pl.BlockSpec((1, ts), lambda i: (0, i)),
                   pl.BlockSpec((CLASS_ROWS, LANES), lambda i: (0, 0))],
        out_shape=[jax.ShapeDtypeStruct((1, n_slots), I32),
                   jax.ShapeDtypeStruct((CLASS_ROWS, LANES), I32)],
        scratch_shapes=[pltpu.VMEM((CLASS_ROWS, LANES), F32)],
        compiler_params=_params("arbitrary"),
        name="slot_ranks",
    )(cls, tri)


def _pos_kernel(off_ref, e_ref, rank_ref, pos_ref):
    e = e_ref[...]
    pos = rank_ref[...]
    for k in range(N_CLASSES):
        pos = pos + jnp.where(e == k, off_ref[k], 0)
    pos_ref[...] = pos


def _positions(off, e_flat, rank):
    n_slots = rank.shape[1]
    ts = min(n_slots, 8192)
    spec = pl.BlockSpec((1, ts), lambda i, *_: (0, i))
    return pl.pallas_call(
        _pos_kernel,
        grid_spec=pltpu.PrefetchScalarGridSpec(
            num_scalar_prefetch=1, grid=(n_slots // ts,), in_specs=[spec, spec], out_specs=spec),
        out_shape=jax.ShapeDtypeStruct((1, n_slots), I32),
        compiler_params=_params("arbitrary"),
        name="slot_positions",
    )(off, e_flat, rank)


def _row_copy(src_ref, r, dst_ref, p, sem):
    return pltpu.make_async_copy(src_ref.at[pl.ds(r, 1)], dst_ref.at[pl.ds(p, 1)], sem)


def _rows_wait(src_ref, dst_ref, sem):
    pltpu.make_async_copy(src_ref, dst_ref, sem).wait()


def _dispatch_kernel(end_ref, cnt_ref, pos_ref, hp_ref, xs_ref, zero_ref, sem, zsem):
    i = pl.program_id(0)
    tm = hp_ref.shape[0]
    tz = zero_ref.shape[0]

    @pl.when(i == 0)
    def _():
        zero_ref[...] = _packed_zeros(zero_ref.shape)
        n_rows = xs_ref.shape[0]
        used = end_ref[N_CLASSES - 1]

        def clear(start):
            return pltpu.make_async_copy(zero_ref, xs_ref.at[pl.ds(pl.multiple_of(start, tz), tz)], zsem)

        for wait in (False, True):
            for e in range(N_CLASSES):
                @pl.when(cnt_ref[e] > 0)
                def _():
                    cp = clear(end_ref[e] - tz)
                    cp.wait() if wait else cp.start()

                @pl.when(used + e * tz < n_rows)
                def _():
                    cp = clear(used + e * tz)
                    cp.wait() if wait else cp.start()

    def issue(h, _):
        for k in range(2):
            r = 2 * h + k
            _row_copy(hp_ref, r, xs_ref, pos_ref[0, 0, r], sem).start(priority=k)
        return 0

    lax.fori_loop(0, tm // 2, issue, 0, unroll=8)
    _rows_wait(hp_ref, xs_ref.at[pl.ds(0, tm)], sem)


def _dispatch(end, cnt, pos3, hp, n_rows):
    t, w = hp.shape
    tm = TM_DISP
    return pl.pallas_call(
        _dispatch_kernel,
        grid_spec=pltpu.PrefetchScalarGridSpec(
            num_scalar_prefetch=2,
            grid=(t // tm,),
            in_specs=[pl.BlockSpec((1, 1, tm), lambda i, *_: (i, 0, 0), memory_space=pltpu.SMEM),
                      pl.BlockSpec((tm, w), lambda i, *_: (i, 0))],
            out_specs=pl.BlockSpec(memory_space=pl.ANY),
            scratch_shapes=[pltpu.VMEM((TM_EXP, w), U32), pltpu.SemaphoreType.DMA(()),
                            pltpu.SemaphoreType.DMA(())]),
        out_shape=jax.ShapeDtypeStruct((n_rows, w), U32),
        compiler_params=_params("arbitrary"),
        name="dispatch",
    )(end, cnt, pos3, hp)


def _expert_kernel(ta_ref, tb_ref, nv_ref, xs_ref, wga_ref, wua_ref, wda_ref, wgb_ref, wub_ref, wdb_ref,
                   ys_ref, unpack_ref, pack_ref):
    i = pl.program_id(0)
    hw = ys_ref.shape[1]

    @pl.when(i < nv_ref[0])
    def _():
        lo, hi = _load_packed_rows(xs_ref, unpack_ref)
        x = jnp.concatenate([lo[:, :hw], hi[:, :hw]], axis=1).astype(BF16)
        gates = lo[:, hw:]

        def mlp(wg_ref, wu_ref, wd_ref):
            g = jnp.dot(x, wg_ref[0].astype(BF16), preferred_element_type=F32)
            u = jnp.dot(x, wu_ref[0].astype(BF16), preferred_element_type=F32)
            he = (_silu(g) * u).astype(BF16)
            return jnp.dot(he, wd_ref[0].astype(BF16), preferred_element_type=F32)

        y = gates[:, 0:1] * mlp(wga_ref, wua_ref, wda_ref) + gates[:, 1:2] * mlp(wgb_ref, wub_ref, wdb_ref)
        _store_packed_rows(ys_ref, pack_ref, y[:, :hw], y[:, hw:])

    @pl.when(i >= nv_ref[0])
    def _():
        ys_ref[...] = _packed_zeros(ys_ref.shape)


def _experts(tile_ea, tile_eb, n_valid, xs, w_gate, w_up, w_down, layer):
    n_rows, w = xs.shape
    tm = TM_EXP
    _, _, d, de = w_gate.shape

    def row_map(i, ta, tb, nv):
        return (jnp.minimum(i, nv[0] - 1), 0)

    def w_spec(shape, first):
        if first:
            return pl.BlockSpec((None,) + shape, lambda i, ta, tb, nv: (layer, ta[i], 0, 0))
        return pl.BlockSpec((None,) + shape, lambda i, ta, tb, nv: (layer, tb[i], 0, 0))

    return pl.pallas_call(
        _expert_kernel,
        grid_spec=pltpu.PrefetchScalarGridSpec(
            num_scalar_prefetch=3,
            grid=(n_rows // tm,),
            in_specs=[pl.BlockSpec((tm, w), row_map),
                      w_spec((1, d, de), True), w_spec((1, d, de), True), w_spec((1, de, d), True),
                      w_spec((1, d, de), False), w_spec((1, d, de), False), w_spec((1, de, d), False)],
            out_specs=pl.BlockSpec((tm, d // 2), lambda i, ta, tb, nv: (i, 0)),
            scratch_shapes=[_pack_scratch(tm, w), _pack_scratch(tm, d // 2)]),
        out_shape=jax.ShapeDtypeStruct((n_rows, d // 2), U32),
        compiler_params=_params("arbitrary"),
        name="experts",
    )(tile_ea, tile_eb, n_valid, xs, w_gate, w_up, w_down, w_gate, w_up, w_down)


def _combine_kernel(pos_ref, pos_next_ref, ys_ref, x1_ref, g2_ref, lng_ref, lnb_ref,
                    o_ref, buf_ref, sems, unpack_ref, *, alpha):
    i = pl.program_id(0)
    tm = x1_ref.shape[0]
    slot = lax.rem(i, 2)

    def gather(p_ref, s):
        def issue(h, _):
            for k in range(2):
                r = 2 * h + k
                _row_copy(ys_ref, p_ref[0, 0, r], buf_ref.at[s], r, sems.at[s]).start(priority=k)
            return 0

        lax.fori_loop(0, tm // 2, issue, 0, unroll=8)

    @pl.when(i == 0)
    def _():
        gather(pos_ref, 0)

    @pl.when(i + 1 < pl.num_programs(0))
    def _():
        gather(pos_next_ref, 1 - slot)

    _rows_wait(ys_ref.at[pl.ds(0, tm)], buf_ref.at[slot], sems.at[slot])
    y = jnp.concatenate(_load_packed_rows(buf_ref.at[slot], unpack_ref), axis=1)
    z = alpha * x1_ref[...] + g2_ref[0] * y
    o_ref[...] = _layer_norm(z) * lng_ref[...] + lnb_ref[...]


def _combine(pos3, ys, x1, g2, ln_g, ln_b, seq, alpha):
    t, d = x1.shape
    w = ys.shape[1]
    tm = TM_DISP
    nb = t // tm
    per_b = seq // tm
    row_spec = pl.BlockSpec((1, d), lambda i: (0, 0))
    return pl.pallas_call(
        functools.partial(_combine_kernel, alpha=alpha),
        grid=(nb,),
        in_specs=[pl.BlockSpec((1, 1, tm), lambda i: (i, 0, 0), memory_space=pltpu.SMEM),
                  pl.BlockSpec((1, 1, tm), lambda i: (jnp.minimum(i + 1, nb - 1), 0, 0),
                               memory_space=pltpu.SMEM),
                  pl.BlockSpec(memory_space=pl.ANY),
                  pl.BlockSpec((tm, d), lambda i: (i, 0)),
                  pl.BlockSpec((1, 1, d), lambda i: (i // per_b, 0, 0)),
                  row_spec, row_spec],
        out_specs=pl.BlockSpec((tm, d), lambda i: (i, 0)),
        out_shape=jax.ShapeDtypeStruct((t, d), F32),
        scratch_shapes=[pltpu.VMEM((2, tm, w), U32), pltpu.SemaphoreType.DMA((2,)), _pack_scratch(tm, w)],
        compiler_params=_params("arbitrary"),
        name="combine",
    )(pos3, pos3, ys, x1, g2, ln_g.reshape(1, d), ln_b.reshape(1, d))


def _class_experts():
    lo, hi = [], []
    for g in range(N_GROUPS):
        for i, j in GROUP_PAIRS:
            lo.append(g * EXPERTS_PER_GROUP + i)
            hi.append(g * EXPERTS_PER_GROUP + j)
    return jnp.array(lo, I32), jnp.array(hi, I32)


def _moe(hp, cls, x1, g2, ln_g, ln_b, w_gate, w_up, w_down, layer, seq, alpha):
    t = hp.shape[0]
    n_tiles = t // TM_EXP + N_CLASSES
    rank, cnt = _ranks(cls)
    counts = cnt[:N_CLASSES, 0]
    tiles_c = (counts + (TM_EXP - 1)) // TM_EXP
    tile_end = jnp.cumsum(tiles_c)
    end = (tile_end * TM_EXP).astype(I32)
    off = end - (tiles_c * TM_EXP).astype(I32)
    n_valid = tile_end[-1:].astype(I32)
    tile_ids = jnp.arange(n_tiles, dtype=I32)
    tile_cls = jnp.sum(tile_ids[:, None] >= tile_end[None, :], axis=1).astype(I32)
    tile_cls = jnp.minimum(tile_cls, tile_cls[jnp.maximum(n_valid[0] - 1, 0)])
    cls_lo, cls_hi = _class_experts()
    nb = t // TM_DISP
    pos3 = _positions(off, cls, rank).reshape(nb, 1, TM_DISP)
    xs = _dispatch(end, counts, pos3, hp, n_tiles * TM_EXP)
    ys = _experts(cls_lo[tile_cls], cls_hi[tile_cls], n_valid, xs, w_gate, w_up, w_down, layer)
    return _combine(pos3, ys, x1, g2, ln_g, ln_b, seq, alpha)


def _rotary_tables(seq):
    half = RET_QK // 2
    inv = 1.0 / (ROPE_BASE ** (jnp.arange(0, RET_QK, 2, dtype=F32) / RET_QK))
    ang = jnp.arange(seq, dtype=F32)[:, None] * inv[None, :]
    cos = jnp.cos(ang)
    sin = jnp.sin(ang)
    reps = LANES // RET_QK
    cos_t = jnp.tile(jnp.concatenate([cos, cos], axis=1), (1, reps))
    sin_t = jnp.tile(jnp.concatenate([-sin, sin], axis=1), (1, reps))
    del half
    return cos_t, sin_t


def kernel(x, c, w_ada, b_ada, w_in, w_out, lambda_q1, lambda_k1, lambda_q2, lambda_k2, diff_subln,
           ln_mix_g, ln_mix_b, ln_ffn_g, ln_ffn_b, router_w, router_b, w_gate, w_up, w_down):
    batch, seq, d = x.shape
    depth = w_ada.shape[0]
    alpha = (2 * depth) ** 0.25
    mod = _adaln(c, w_ada, b_ada)
    cos_t, sin_t = _rotary_tables(seq)
    rw_t = router_w.T.astype(BF16)
    w_in_bf = w_in.astype(BF16)
    w_vt_bf = jnp.swapaxes(w_in[:, :, (N_SEG - 1) * SEG:], 1, 2).astype(BF16)
    w_out_bf = w_out.astype(BF16)
    xf = x.reshape(batch * seq, d)
    for l in range(depth):
        m = mod[l].reshape(batch, N_MOD, 1, d)
        sh1, sc1, g1, sh2, sc2, g2 = (m[:, i] for i in range(N_MOD))
        rq, rk, rv, rg, dq, dk, dvt = _inproj(xf, sc1, sh1, w_in_bf, w_vt_bf, l, cos_t, sin_t, seq)
        ret = _retention(rq, rk, rv, rg, batch, seq)
        lambda_init = 0.8 - 0.6 * math.exp(-0.3 * l)
        dif = _diff_attention(dq, dk, dvt, lambda_q1[l], lambda_k1[l], lambda_q2[l], lambda_k2[l],
                              diff_subln[l], lambda_init, batch, seq)
        x1, hp, cls = _outproj(ret, dif, xf, g1, sc2, sh2, w_out_bf, l,
                               ln_mix_g[l], ln_mix_b[l], rw_t, router_b, seq, alpha)
        xf = _moe(hp, cls, x1, g2, ln_ffn_g[l], ln_ffn_b[l], w_gate, w_up, w_down, l, seq, alpha)
    return xf.reshape(batch, seq, d)
```

```python
import functools
import math

import jax
import jax.numpy as jnp
from jax import lax
from jax.experimental import pallas as pl
from jax.experimental.pallas import tpu as pltpu

F32 = jnp.float32
BF16 = jnp.bfloat16
U32 = jnp.uint32
I32 = jnp.int32

RET_HEADS = 8
RET_QK = 64
DIFF_HEADS = 4
DIFF_QK = 64
HEAD_PAIR = 128
SEG = 512
N_SEG = 7
CHUNK = 128
ROPE_BASE = 10000.0
SUBLN_EPS = 1e-5
LN_EPS = 1e-5
N_EXPERTS = 16
N_GROUPS = 4
EXPERTS_PER_GROUP = 4
TOP_K = 2
GROUP_PAIRS = tuple((i, j) for i in range(EXPERTS_PER_GROUP) for j in range(i + 1, EXPERTS_PER_GROUP))
N_CLASSES = N_GROUPS * len(GROUP_PAIRS)
CLASS_ROWS = 32
N_MOD = 6
LANES = 128
BF16_SUBLANES = 16
VMEM_LIMIT = 56 * 1024 * 1024

TM_PROJ = 1024
RET_GROUP = 8
CT = 256
BK = 512
TS_RANK = 512
TM_DISP = 512
TM_EXP = 512


def _params(*sem):
    return pltpu.CompilerParams(dimension_semantics=sem, vmem_limit_bytes=VMEM_LIMIT)


def _layer_norm(x):
    mu = jnp.mean(x, axis=-1, keepdims=True)
    xc = x - mu
    var = jnp.mean(xc * xc, axis=-1, keepdims=True)
    return xc * lax.rsqrt(var + LN_EPS)


def _silu(x):
    return x * jax.nn.sigmoid(x)


def _adaln_kernel(c_ref, w_ref, b_ref, o_ref):
    cond = _silu(c_ref[...]).astype(BF16)
    o_ref[0] = jnp.dot(cond, w_ref[0].astype(BF16), preferred_element_type=F32) + b_ref[0]


def _adaln(c, w_ada, b_ada):
    depth, d, n = w_ada.shape
    b = c.shape[0]
    tn = 1536
    return pl.pallas_call(
        _adaln_kernel,
        grid=(depth, n // tn),
        in_specs=[pl.BlockSpec((b, d), lambda l, j: (0, 0)),
                  pl.BlockSpec((1, d, tn), lambda l, j: (l, 0, j)),
                  pl.BlockSpec((1, 1, tn), lambda l, j: (l, 0, j))],
        out_specs=pl.BlockSpec((1, b, tn), lambda l, j: (l, 0, j)),
        out_shape=jax.ShapeDtypeStruct((depth, b, n), F32),
        compiler_params=_params("arbitrary", "arbitrary"),
        name="adaln",
    )(c, w_ada, b_ada.reshape(depth, 1, n))


def _inproj_kernel(x_ref, sc_ref, sh_ref, w_ref, wvt_ref, cos_ref, sin_ref, *out_refs):
    _inproj_body(x_ref[...], sc_ref, sh_ref, w_ref, wvt_ref, cos_ref, sin_ref, out_refs)


def _inproj_body(x, sc_ref, sh_ref, w_ref, wvt_ref, cos_ref, sin_ref, out_refs):
    h = (_layer_norm(x) * (1.0 + sc_ref[0]) + sh_ref[0]).astype(BF16)
    cos = cos_ref[...]
    sin = sin_ref[...]
    lane = lax.broadcasted_iota(I32, cos.shape, 1)
    first_half = (lane & (RET_QK - 1)) < (RET_QK // 2)

    def rotary(p):
        outs = []
        for c in range(SEG // LANES):
            pc = p[:, c * LANES:(c + 1) * LANES]
            swapped = jnp.where(first_half, pltpu.roll(pc, LANES - RET_QK // 2, 1),
                                pltpu.roll(pc, RET_QK // 2, 1))
            outs.append(pc * cos + swapped * sin)
        return jnp.concatenate(outs, axis=1)

    for s, o_ref in enumerate(out_refs[:-1]):
        p = jnp.dot(h, w_ref[:, s * SEG:(s + 1) * SEG], preferred_element_type=F32)
        if s in (0, 1):
            p = rotary(p)
        if s == 0:
            p = p * (RET_QK ** -0.5)
        if s == 4:
            p = p * (DIFF_QK ** -0.5 * math.log2(math.e))
        o_ref[...] = p.astype(BF16)
    out_refs[-1][...] = lax.dot_general(wvt_ref[...], h, (((1,), (1,)), ((), ())),
                                        preferred_element_type=F32).astype(BF16)


def _inproj(x2d, sc, sh, w_all, w_vt_all, layer, cos_t, sin_t, seq):
    t, d = x2d.shape
    tm = TM_PROJ
    per_b = seq // tm
    n_main = (N_SEG - 1) * SEG
    return pl.pallas_call(
        _inproj_kernel,
        grid=(t // tm,),
        in_specs=[pl.BlockSpec((tm, d), lambda i: (i, 0)),
                  pl.BlockSpec((1, 1, d), lambda i: (i // per_b, 0, 0)),
                  pl.BlockSpec((1, 1, d), lambda i: (i // per_b, 0, 0)),
                  pl.BlockSpec((None, d, n_main), lambda i: (layer, 0, 0)),
                  pl.BlockSpec((None, SEG, d), lambda i: (layer, 0, 0)),
                  pl.BlockSpec((tm, LANES), lambda i: (i % per_b, 0)),
                  pl.BlockSpec((tm, LANES), lambda i: (i % per_b, 0))],
        out_specs=[pl.BlockSpec((tm, SEG), lambda i: (i, 0))] * (N_SEG - 1)
        + [pl.BlockSpec((SEG, tm), lambda i: (0, i))],
        out_shape=[jax.ShapeDtypeStruct((t, SEG), BF16)] * (N_SEG - 1)
        + [jax.ShapeDtypeStruct((SEG, t), BF16)],
        compiler_params=_params("arbitrary"),
        name="inproj",
    )(x2d, sc, sh, w_all, w_vt_all, cos_t, sin_t)


def _ret_kernel(q_ref, k_ref, v_ref, g_ref, d2_ref, qd_ref, kd_ref, gm_ref, o_ref, *, n_chunks):
    c = CHUNK
    lo = lax.broadcasted_iota(I32, (c, HEAD_PAIR), 1) < RET_QK
    r = lax.broadcasted_iota(I32, (HEAD_PAIR, HEAD_PAIR), 0) < RET_QK
    cc = lax.broadcasted_iota(I32, (HEAD_PAIR, HEAD_PAIR), 1) < RET_QK
    same_head = r == cc
    d2 = d2_ref[0]
    qd = qd_ref[0]
    kd = kd_ref[0]
    gm = gm_ref[0]

    def split_heads(a):
        zero = jnp.zeros_like(a)
        return jnp.concatenate([jnp.where(lo, a, zero), jnp.where(lo, zero, a)], axis=0)

    grp = RET_GROUP

    def body(t, state):
        sls = [pl.ds(pl.multiple_of((t * grp + g) * c, c), c) for g in range(grp)]
        qs = [q_ref[sl, :] for sl in sls]
        ks = [k_ref[sl, :] for sl in sls]
        vs = [v_ref[sl, :] for sl in sls]
        s2s = [lax.dot_general(split_heads(q), k, (((1,), (1,)), ((), ())), preferred_element_type=F32)
               for q, k in zip(qs, ks)]
        kvs = [lax.dot_general((k.astype(F32) * kd).astype(BF16), v, (((0,), (0,)), ((), ())),
                               preferred_element_type=F32) for k, v in zip(ks, vs)]
        states = []
        for kv in kvs:
            states.append(state)
            state = state * gm + jnp.where(same_head, kv, 0.0)
        outs = []
        for q, v, s2, st in zip(qs, vs, s2s, states):
            p2 = (s2 * d2).astype(BF16)
            pcat = jnp.concatenate([p2[:c], p2[c:]], axis=1)
            inner = jnp.dot(pcat, split_heads(v), preferred_element_type=F32)
            qdq = (q.astype(F32) * qd).astype(BF16)
            outs.append(inner + jnp.dot(qdq, st.astype(BF16), preferred_element_type=F32))
        inv = 1.0 / RET_QK
        for sl, o in zip(sls, outs):
            s_all = jnp.sum(o, axis=-1, keepdims=True)
            s_lo = jnp.sum(jnp.where(lo, o, 0.0), axis=-1, keepdims=True)
            dlt = o - jnp.where(lo, s_lo, s_all - s_lo) * inv
            dd = dlt * dlt
            v_all = jnp.sum(dd, axis=-1, keepdims=True)
            v_lo = jnp.sum(jnp.where(lo, dd, 0.0), axis=-1, keepdims=True)
            var = jnp.where(lo, v_lo, v_all - v_lo) * inv
            y = dlt * lax.rsqrt(var + LN_EPS)
            o_ref[sl, :] = (y * _silu(g_ref[sl, :].astype(F32))).astype(BF16)
        return state

    lax.fori_loop(0, n_chunks // grp, body, jnp.zeros((HEAD_PAIR, HEAD_PAIR), F32))


def _retention_tables():
    c = CHUNK
    log_g = jnp.log(1.0 - 2.0 ** (-5.0 - jnp.arange(RET_HEADS, dtype=F32)))
    idx = jnp.arange(c, dtype=F32)
    rel = idx[:, None] - idx[None, :]
    decay = jnp.where(rel >= 0, jnp.exp(log_g[:, None, None] * jnp.maximum(rel, 0.0)), 0.0)
    d2 = decay.reshape(RET_HEADS // 2, 2 * c, c)
    lane_head = jnp.arange(HEAD_PAIR) // RET_QK
    pair_log = log_g.reshape(RET_HEADS // 2, 2)[:, lane_head]
    qd = jnp.exp(pair_log[:, None, :] * (idx + 1.0)[None, :, None])
    kd = jnp.exp(pair_log[:, None, :] * (c - 1.0 - idx)[None, :, None])
    gm = jnp.broadcast_to(jnp.exp(pair_log * c)[:, :, None], (RET_HEADS // 2, HEAD_PAIR, HEAD_PAIR))
    return d2, qd, kd, gm


def _retention(rq, rk, rv, rg, batch, seq):
    t = rq.shape[0]
    n_pairs = RET_HEADS // 2
    d2, qd, kd, gm = _retention_tables()
    seq_spec = pl.BlockSpec((seq, HEAD_PAIR), lambda b, p: (b, p))
    return pl.pallas_call(
        functools.partial(_ret_kernel, n_chunks=seq // CHUNK),
        grid=(batch, n_pairs),
        in_specs=[seq_spec, seq_spec, seq_spec, seq_spec,
                  pl.BlockSpec((1, 2 * CHUNK, CHUNK), lambda b, p: (p, 0, 0)),
                  pl.BlockSpec((1, CHUNK, HEAD_PAIR), lambda b, p: (p, 0, 0)),
                  pl.BlockSpec((1, CHUNK, HEAD_PAIR), lambda b, p: (p, 0, 0)),
                  pl.BlockSpec((1, HEAD_PAIR, HEAD_PAIR), lambda b, p: (p, 0, 0))],
        out_specs=seq_spec,
        out_shape=jax.ShapeDtypeStruct((t, SEG), BF16),
        compiler_params=_params("arbitrary", "arbitrary"),
        name="retention",
    )(rq, rk, rv, rg, d2, qd, kd, gm)


def _diff_kernel(lq1_ref, lk1_ref, lq2_ref, lk2_ref, sg_ref, q_ref, k_ref, vt_ref, o_ref,
                 vta_ref, sa_ref, sb_ref, mx_ref, m_ref, acc_ref, *, lambda_init):
    bq, bk, ct = q_ref.shape[0], BK, CT
    n_ct = 2 * bq // ct
    s_refs = (sa_ref, sb_ref)

    vta_ref[:HEAD_PAIR, :] = vt_ref[...]
    rid = lax.broadcasted_iota(I32, (vta_ref.shape[0] - HEAD_PAIR, vta_ref.shape[1]), 0)
    vta_ref[HEAD_PAIR:, :] = jnp.where(rid == 0, 1.0, 0.0).astype(BF16)

    lam = (jnp.exp(jnp.sum(lq1_ref[...] * lk1_ref[...], axis=-1, keepdims=True))
           - jnp.exp(jnp.sum(lq2_ref[...] * lk2_ref[...], axis=-1, keepdims=True)) + lambda_init)
    q = q_ref[...]
    lo = lax.broadcasted_iota(I32, q.shape, 1) < DIFF_QK
    zero = jnp.zeros_like(q)
    q2 = jnp.concatenate([jnp.where(lo, q, zero), jnp.where(lo, zero, q)], axis=0)
    q_tiles = [q2[c * ct:(c + 1) * ct, :] for c in range(n_ct)]

    def block_modes(j):
        modes = []
        for c in range(n_ct):
            q_lo = (c * ct) % bq
            if j * bk >= q_lo + ct:
                modes.append("skip")
            elif (j + 1) * bk - 1 <= q_lo:
                modes.append("full")
            else:
                modes.append("masked")
        return tuple(modes)

    def scores(j, slot, modes):
        k_blk = k_ref[j * bk:(j + 1) * bk, :]
        for c, qt in enumerate(q_tiles):
            if modes[c] == "skip":
                continue
            cols = slice(c * ct, (c + 1) * ct)
            s = lax.dot_general(k_blk, qt, (((1,), (1,)), ((), ())), preferred_element_type=F32)
            if modes[c] == "masked":
                kpos = j * bk + lax.broadcasted_iota(I32, s.shape, 0)
                qpos = (c * ct) % bq + lax.broadcasted_iota(I32, s.shape, 1)
                s = jnp.where(kpos <= qpos, s, -jnp.inf)
            s_refs[slot][:, cols] = s
            mx_ref[slot:slot + 1, cols] = jnp.max(s, axis=0, keepdims=True)

    def update(j, slot, modes):
        vta_blk = vta_ref[:, j * bk:(j + 1) * bk]
        for c in range(n_ct):
            if modes[c] == "skip":
                continue
            cols = slice(c * ct, (c + 1) * ct)
            m = m_ref[:, cols]
            m_new = jnp.maximum(m, mx_ref[slot:slot + 1, cols])
            p = jnp.exp2(s_refs[slot][:, cols] - m_new).astype(BF16)
            acc_ref[:, cols] = (jnp.exp2(m - m_new) * acc_ref[:, cols]
                                + jnp.dot(vta_blk, p, preferred_element_type=F32))
            m_ref[:, cols] = m_new

    m_ref[...] = jnp.full(m_ref.shape, -jnp.inf, F32)
    acc_ref[...] = jnp.zeros(acc_ref.shape, F32)
    n_blocks = bq // bk
    scores(0, 0, block_modes(0))
    for j in range(n_blocks):
        if j + 1 < n_blocks:
            scores(j + 1, (j + 1) % 2, block_modes(j + 1))
        update(j, j % 2, block_modes(j))
    acc = acc_ref[...]
    a = acc[:HEAD_PAIR] / acc[HEAD_PAIR:HEAD_PAIR + 1]
    out = (a[:, :bq] - lam * a[:, bq:]).T
    ms = jnp.mean(out * out, axis=-1, keepdims=True)
    out = out * lax.rsqrt(ms + SUBLN_EPS) * sg_ref[...] * (1.0 - lambda_init)
    o_ref[...] = out.astype(BF16)


def _diff_attention(dq, dk, dvt, lq1, lk1, lq2, lk2, subln, lambda_init, batch, seq):
    t = dq.shape[0]
    lam_spec = pl.BlockSpec((1, DIFF_QK), lambda b, h: (0, 0))
    seq_spec = pl.BlockSpec((seq, HEAD_PAIR), lambda b, h: (b, h))
    n_rows = HEAD_PAIR + BF16_SUBLANES
    return pl.pallas_call(
        functools.partial(_diff_kernel, lambda_init=lambda_init),
        grid=(batch, DIFF_HEADS),
        in_specs=[lam_spec, lam_spec, lam_spec, lam_spec,
                  pl.BlockSpec((1, HEAD_PAIR), lambda b, h: (0, 0)),
                  seq_spec, seq_spec,
                  pl.BlockSpec((HEAD_PAIR, seq), lambda b, h: (h, b))],
        out_specs=seq_spec,
        out_shape=jax.ShapeDtypeStruct((t, SEG), BF16),
        scratch_shapes=[pltpu.VMEM((n_rows, seq), BF16),
                        pltpu.VMEM((BK, 2 * seq), F32), pltpu.VMEM((BK, 2 * seq), F32),
                        pltpu.VMEM((2, 2 * seq), F32), pltpu.VMEM((1, 2 * seq), F32),
                        pltpu.VMEM((n_rows, 2 * seq), F32)],
        compiler_params=_params("arbitrary", "arbitrary"),
        name="diff_attention",
    )(lq1.reshape(1, -1), lk1.reshape(1, -1), lq2.reshape(1, -1), lk2.reshape(1, -1),
      subln.reshape(1, -1), dq, dk, dvt)


def _pack_scratch(rows, hw):
    return pltpu.VMEM((hw // LANES, 2 * rows, LANES), F32)


def _packed_zeros(shape):
    rows, hw = shape
    return pltpu.bitcast(jnp.zeros((2 * rows, hw), BF16), U32)


def _store_packed_rows(ref, scr, lo, hi):
    rows, hw = ref.shape
    for c in range(hw // LANES):
        cols = slice(c * LANES, (c + 1) * LANES)
        scr[c, pl.ds(0, rows, stride=2), :] = lo[:, cols]
        scr[c, pl.ds(1, rows, stride=2), :] = hi[:, cols]
    for c in range(hw // LANES):
        ref[:, c * LANES:(c + 1) * LANES] = pltpu.bitcast(scr[c].astype(BF16), U32)


def _load_packed_rows(ref, scr):
    rows, hw = ref.shape
    for c in range(hw // LANES):
        scr[c] = pltpu.bitcast(ref[:, c * LANES:(c + 1) * LANES], BF16).astype(F32)
    lo = jnp.concatenate([scr[c, pl.ds(0, rows, stride=2), :] for c in range(hw // LANES)], axis=1)
    hi = jnp.concatenate([scr[c, pl.ds(1, rows, stride=2), :] for c in range(hw // LANES)], axis=1)
    return lo, hi


def _first_top2(vals):
    m1 = jnp.maximum(jnp.maximum(vals[0], vals[1]), jnp.maximum(vals[2], vals[3]))
    i1 = jnp.where(vals[0] == m1, 0, jnp.where(vals[1] == m1, 1, jnp.where(vals[2] == m1, 2, 3)))
    rest = [jnp.where(i1 == i, -1.0, v) for i, v in enumerate(vals)]
    m2 = jnp.maximum(jnp.maximum(rest[0], rest[1]), jnp.maximum(rest[2], rest[3]))
    i2 = jnp.where(rest[0] == m2, 0, jnp.where(rest[1] == m2, 1, jnp.where(rest[2] == m2, 2, 3)))
    return m1, i1, m2, i2


def _outproj_kernel(ret_ref, dif_ref, x_ref, g1_ref, sc2_ref, sh2_ref, wo_ref, lng_ref, lnb_ref,
                    rw_ref, rb_ref, x1_ref, hp_ref, cls_ref, pack_ref, *, alpha):
    half = ret_ref.shape[1]
    hw = x_ref.shape[1] // 2
    mixed = (jnp.dot(ret_ref[...], wo_ref[:half, :], preferred_element_type=F32)
             + jnp.dot(dif_ref[...], wo_ref[half:, :], preferred_element_type=F32))
    x1 = _layer_norm(alpha * x_ref[...] + g1_ref[0] * mixed) * lng_ref[...] + lnb_ref[...]
    x1_ref[...] = x1
    h2 = _layer_norm(x1) * (1.0 + sc2_ref[0]) + sh2_ref[0]
    logits = lax.dot_general(rw_ref[...], h2.astype(BF16), (((1,), (1,)), ((), ())),
                             preferred_element_type=F32) + rb_ref[...]
    e = jnp.exp(logits - jnp.max(logits, axis=0, keepdims=True))
    probs = e / jnp.sum(e, axis=0, keepdims=True)
    rows = [probs[i:i + 1, :] for i in range(N_EXPERTS)]
    group_tops = []
    for g in range(N_GROUPS):
        m1, _, m2, _ = _first_top2(rows[g * EXPERTS_PER_GROUP:(g + 1) * EXPERTS_PER_GROUP])
        group_tops.append(m1 + m2)
    gmax = jnp.maximum(jnp.maximum(group_tops[0], group_tops[1]), jnp.maximum(group_tops[2], group_tops[3]))
    gbest = jnp.where(group_tops[0] == gmax, 0,
                      jnp.where(group_tops[1] == gmax, 1, jnp.where(group_tops[2] == gmax, 2, 3)))
    sel = []
    for i in range(EXPERTS_PER_GROUP):
        v = rows[3 * EXPERTS_PER_GROUP + i]
        for g in (2, 1, 0):
            v = jnp.where(gbest == g, rows[g * EXPERTS_PER_GROUP + i], v)
        sel.append(v)
    p0, i0, p1, i1 = _first_top2(sel)
    denom = p0 + p1
    first_is_low = i0 < i1
    i_lo = jnp.where(first_is_low, i0, i1)
    i_hi = jnp.where(first_is_low, i1, i0)
    pair_base = jnp.where(i_lo == 0, 0, jnp.where(i_lo == 1, 3, 5))
    cls_ref[...] = gbest * len(GROUP_PAIRS) + pair_base + (i_hi - i_lo - 1)
    g_lo = jnp.where(first_is_low, p0, p1) / denom
    g_hi = jnp.where(first_is_low, p1, p0) / denom
    tm = x1.shape[0]
    rid = lax.broadcasted_iota(I32, (LANES, tm), 0)
    gcols = jnp.where(rid == 0, g_lo, jnp.where(rid == 1, g_hi, 0.0)).T
    _store_packed_rows(hp_ref, pack_ref, jnp.concatenate([h2[:, :hw], gcols], axis=1),
                       jnp.concatenate([h2[:, hw:], gcols], axis=1))


def _outproj(ret, dif, x2d, g1, sc2, sh2, wo_all, layer, ln_g, ln_b, rw_t, rb, seq, alpha):
    t, d = x2d.shape
    tm = TM_PROJ
    per_b = seq // tm
    mod_spec = pl.BlockSpec((1, 1, d), lambda i: (i // per_b, 0, 0))
    row_spec = pl.BlockSpec((1, d), lambda i: (0, 0))
    return pl.pallas_call(
        functools.partial(_outproj_kernel, alpha=alpha),
        grid=(t // tm,),
        in_specs=[pl.BlockSpec((tm, SEG), lambda i: (i, 0)),
                  pl.BlockSpec((tm, SEG), lambda i: (i, 0)),
                  pl.BlockSpec((tm, d), lambda i: (i, 0)),
                  mod_spec, mod_spec, mod_spec,
                  pl.BlockSpec((None, 2 * SEG, d), lambda i: (layer, 0, 0)),
                  row_spec, row_spec,
                  pl.BlockSpec((N_EXPERTS, d), lambda i: (0, 0)),
                  pl.BlockSpec((N_EXPERTS, 1), lambda i: (0, 0))],
        out_specs=[pl.BlockSpec((tm, d), lambda i: (i, 0)),
                   pl.BlockSpec((tm, d // 2 + LANES), lambda i: (i, 0)),
                   pl.BlockSpec((1, tm), lambda i: (0, i))],
        out_shape=[jax.ShapeDtypeStruct((t, d), F32),
                   jax.ShapeDtypeStruct((t, d // 2 + LANES), U32),
                   jax.ShapeDtypeStruct((1, t), I32)],
        scratch_shapes=[_pack_scratch(tm, d // 2 + LANES)],
        compiler_params=_params("arbitrary"),
        name="outproj_router",
    )(ret, dif, x2d, g1, sc2, sh2, wo_all, ln_g.reshape(1, d), ln_b.reshape(1, d), rw_t, rb.reshape(-1, 1))


def _rank_kernel(e_ref, tri_ref, rank_ref, cnt_ref, carry_ref):
    i = pl.program_id(0)

    @pl.when(i == 0)
    def _():
        carry_ref[...] = jnp.zeros_like(carry_ref)

    e = e_ref[...]
    ts = e.shape[1]
    onehot = lax.broadcasted_iota(I32, (CLASS_ROWS, ts), 0) == e
    cum = jnp.dot(jnp.where(onehot, 1.0, 0.0).astype(BF16), tri_ref[...],
                  preferred_element_type=F32)
    carry = carry_ref[...]
    total = cum + carry[:, 0:1]
    rank_ref[...] = (jnp.sum(jnp.where(onehot, total, 0.0), axis=0, keepdims=True) - 1.0).astype(I32)
    new_carry = carry + cum[:, ts - 1:ts]
    carry_ref[...] = new_carry
    cnt_ref[...] = new_carry.astype(I32)


def _ranks(cls):
    n_slots = cls.shape[1]
    ts = TS_RANK
    tri = (jnp.arange(ts)[:, None] <= jnp.arange(ts)[None, :]).astype(BF16)
    return pl.pallas_call(
        _rank_kernel,
        grid=(n_slots // ts,),
        in_specs=[pl.BlockSpec((1, ts), lambda i: (0, i)),
                  pl.BlockSpec((ts, ts), lambda i: (0, 0))],
        out_specs=[pl.BlockSpec((1, ts), lambda i: (0, i)),
                   pl.BlockSpec((CLASS_ROWS, LANES), lambda i: (0, 0))],
        out_shape=[jax.ShapeDtypeStruct((1, n_slots), I32),
                   jax.ShapeDtypeStruct((CLASS_ROWS, LANES), I32)],
        scratch_shapes=[pltpu.VMEM((CLASS_ROWS, LANES), F32)],
        compiler_params=_params("arbitrary"),
        name="slot_ranks",
    )(cls, tri)


def _pos_kernel(off_ref, e_ref, rank_ref, pos_ref):
    e = e_ref[...]
    pos = rank_ref[...]
    for k in range(N_CLASSES):
        pos = pos + jnp.where(e == k, off_ref[k], 0)
    pos_ref[...] = pos


def _positions(off, e_flat, rank):
    n_slots = rank.shape[1]
    ts = min(n_slots, 8192)
    spec = pl.BlockSpec((1, ts), lambda i, *_: (0, i))
    return pl.pallas_call(
        _pos_kernel,
        grid_spec=pltpu.PrefetchScalarGridSpec(
            num_scalar_prefetch=1, grid=(n_slots // ts,), in_specs=[spec, spec], out_specs=spec),
        out_shape=jax.ShapeDtypeStruct((1, n_slots), I32),
        compiler_params=_params("arbitrary"),
        name="slot_positions",
    )(off, e_flat, rank)


def _row_copy(src_ref, r, dst_ref, p, sem):
    return pltpu.make_async_copy(src_ref.at[pl.ds(r, 1)], dst_ref.at[pl.ds(p, 1)], sem)


def _rows_wait(src_ref, dst_ref, sem):
    pltpu.make_async_copy(src_ref, dst_ref, sem).wait()


def _dispatch_kernel(end_ref, cnt_ref, pos_ref, hp_ref, xs_ref, zero_ref, sem, zsem):
    i = pl.program_id(0)
    tm = hp_ref.shape[0]
    tz = zero_ref.shape[0]

    @pl.when(i == 0)
    def _():
        zero_ref[...] = _packed_zeros(zero_ref.shape)
        n_rows = xs_ref.shape[0]
        used = end_ref[N_CLASSES - 1]

        def clear(start):
            return pltpu.make_async_copy(zero_ref, xs_ref.at[pl.ds(pl.multiple_of(start, tz), tz)], zsem)

        for wait in (False, True):
            for e in range(N_CLASSES):
                @pl.when(cnt_ref[e] > 0)
                def _():
                    cp = clear(end_ref[e] - tz)
                    cp.wait() if wait else cp.start()

                @pl.when(used + e * tz < n_rows)
                def _():
                    cp = clear(used + e * tz)
                    cp.wait() if wait else cp.start()

    def issue(h, _):
        for k in range(2):
            r = 2 * h + k
            _row_copy(hp_ref, r, xs_ref, pos_ref[0, 0, r], sem).start(priority=k)
        return 0

    lax.fori_loop(0, tm // 2, issue, 0, unroll=8)
    _rows_wait(hp_ref, xs_ref.at[pl.ds(0, tm)], sem)


def _dispatch(end, cnt, pos3, hp, n_rows):
    t, w = hp.shape
    tm = TM_DISP
    return pl.pallas_call(
        _dispatch_kernel,
        grid_spec=pltpu.PrefetchScalarGridSpec(
            num_scalar_prefetch=2,
            grid=(t // tm,),
            in_specs=[pl.BlockSpec((1, 1, tm), lambda i, *_: (i, 0, 0), memory_space=pltpu.SMEM),
                      pl.BlockSpec((tm, w), lambda i, *_: (i, 0))],
            out_specs=pl.BlockSpec(memory_space=pl.ANY),
            scratch_shapes=[pltpu.VMEM((TM_EXP, w), U32), pltpu.SemaphoreType.DMA(()),
                            pltpu.SemaphoreType.DMA(())]),
        out_shape=jax.ShapeDtypeStruct((n_rows, w), U32),
        compiler_params=_params("arbitrary"),
        name="dispatch",
    )(end, cnt, pos3, hp)


def _expert_kernel(ta_ref, tb_ref, nv_ref, xs_ref, wga_ref, wua_ref, wda_ref, wgb_ref, wub_ref, wdb_ref,
                   ys_ref, unpack_ref, pack_ref):
    i = pl.program_id(0)
    hw = ys_ref.shape[1]

    @pl.when(i < nv_ref[0])
    def _():
        lo, hi = _load_packed_rows(xs_ref, unpack_ref)
        x = jnp.concatenate([lo[:, :hw], hi[:, :hw]], axis=1).astype(BF16)
        gates = lo[:, hw:]

        def mlp(wg_ref, wu_ref, wd_ref):
            g = jnp.dot(x, wg_ref[0].astype(BF16), preferred_element_type=F32)
            u = jnp.dot(x, wu_ref[0].astype(BF16), preferred_element_type=F32)
            he = (_silu(g) * u).astype(BF16)
            return jnp.dot(he, wd_ref[0].astype(BF16), preferred_element_type=F32)

        y = gates[:, 0:1] * mlp(wga_ref, wua_ref, wda_ref) + gates[:, 1:2] * mlp(wgb_ref, wub_ref, wdb_ref)
        _store_packed_rows(ys_ref, pack_ref, y[:, :hw], y[:, hw:])

    @pl.when(i >= nv_ref[0])
    def _():
        ys_ref[...] = _packed_zeros(ys_ref.shape)


def _experts(tile_ea, tile_eb, n_valid, xs, w_gate, w_up, w_down, layer):
    n_rows, w = xs.shape
    tm = TM_EXP
    _, _, d, de = w_gate.shape

    def row_map(i, ta, tb, nv):
        return (jnp.minimum(i, nv[0] - 1), 0)

    def w_spec(shape, first):
        if first:
            return pl.BlockSpec((None,) + shape, lambda i, ta, tb, nv: (layer, ta[i], 0, 0))
        return pl.BlockSpec((None,) + shape, lambda i, ta, tb, nv: (layer, tb[i], 0, 0))

    return pl.pallas_call(
        _expert_kernel,
        grid_spec=pltpu.PrefetchScalarGridSpec(
            num_scalar_prefetch=3,
            grid=(n_rows // tm,),
            in_specs=[pl.BlockSpec((tm, w), row_map),
                      w_spec((1, d, de), True), w_spec((1, d, de), True), w_spec((1, de, d), True),
                      w_spec((1, d, de), False), w_spec((1, d, de), False), w_spec((1, de, d), False)],
            out_specs=pl.BlockSpec((tm, d // 2), lambda i, ta, tb, nv: (i, 0)),
            scratch_shapes=[_pack_scratch(tm, w), _pack_scratch(tm, d // 2)]),
        out_shape=jax.ShapeDtypeStruct((n_rows, d // 2), U32),
        compiler_params=_params("arbitrary"),
        name="experts",
    )(tile_ea, tile_eb, n_valid, xs, w_gate, w_up, w_down, w_gate, w_up, w_down)


def _combine_kernel(pos_ref, pos_next_ref, ys_ref, x1_ref, g2_ref, lng_ref, lnb_ref,
                    o_ref, buf_ref, sems, unpack_ref, *, alpha):
    o_ref[...] = _combine_body(pos_ref, pos_next_ref, ys_ref, x1_ref, g2_ref, lng_ref, lnb_ref,
                               buf_ref, sems, unpack_ref, alpha)


def _combine_inproj_kernel(pos_ref, pos_next_ref, ys_ref, x1_ref, g2_ref, lng_ref, lnb_ref,
                           sc_ref, sh_ref, w_ref, wvt_ref, cos_ref, sin_ref,
                           o_ref, *rest, alpha):
    out_refs, (buf_ref, sems, unpack_ref) = rest[:N_SEG], rest[N_SEG:]
    x2 = _combine_body(pos_ref, pos_next_ref, ys_ref, x1_ref, g2_ref, lng_ref, lnb_ref,
                       buf_ref, sems, unpack_ref, alpha)
    o_ref[...] = x2
    _inproj_body(x2, sc_ref, sh_ref, w_ref, wvt_ref, cos_ref, sin_ref, out_refs)


def _combine_body(pos_ref, pos_next_ref, ys_ref, x1_ref, g2_ref, lng_ref, lnb_ref,
                  buf_ref, sems, unpack_ref, alpha):
    i = pl.program_id(0)
    tm = x1_ref.shape[0]
    slot = lax.rem(i, 2)

    def gather(p_ref, s):
        def issue(h, _):
            for k in range(2):
                r = 2 * h + k
                _row_copy(ys_ref, p_ref[0, 0, r], buf_ref.at[s], r, sems.at[s]).start(priority=k)
            return 0

        lax.fori_loop(0, tm // 2, issue, 0, unroll=8)

    @pl.when(i == 0)
    def _():
        gather(pos_ref, 0)

    @pl.when(i + 1 < pl.num_programs(0))
    def _():
        gather(pos_next_ref, 1 - slot)

    _rows_wait(ys_ref.at[pl.ds(0, tm)], buf_ref.at[slot], sems.at[slot])
    y = jnp.concatenate(_load_packed_rows(buf_ref.at[slot], unpack_ref), axis=1)
    z = alpha * x1_ref[...] + g2_ref[0] * y
    return _layer_norm(z) * lng_ref[...] + lnb_ref[...]


def _combine(pos3, ys, x1, g2, ln_g, ln_b, seq, alpha):
    t, d = x1.shape
    w = ys.shape[1]
    tm = TM_DISP
    nb = t // tm
    per_b = seq // tm
    row_spec = pl.BlockSpec((1, d), lambda i: (0, 0))
    return pl.pallas_call(
        functools.partial(_combine_kernel, alpha=alpha),
        grid=(nb,),
        in_specs=[pl.BlockSpec((1, 1, tm), lambda i: (i, 0, 0), memory_space=pltpu.SMEM),
                  pl.BlockSpec((1, 1, tm), lambda i: (jnp.minimum(i + 1, nb - 1), 0, 0),
                               memory_space=pltpu.SMEM),
                  pl.BlockSpec(memory_space=pl.ANY),
                  pl.BlockSpec((tm, d), lambda i: (i, 0)),
                  pl.BlockSpec((1, 1, d), lambda i: (i // per_b, 0, 0)),
                  row_spec, row_spec],
        out_specs=pl.BlockSpec((tm, d), lambda i: (i, 0)),
        out_shape=jax.ShapeDtypeStruct((t, d), F32),
        scratch_shapes=[pltpu.VMEM((2, tm, w), U32), pltpu.SemaphoreType.DMA((2,)), _pack_scratch(tm, w)],
        compiler_params=_params("arbitrary"),
        name="combine",
    )(pos3, pos3, ys, x1, g2, ln_g.reshape(1, d), ln_b.reshape(1, d))


def _combine_inproj(pos3, ys, x1, g2, ln_g, ln_b, sc, sh, w_all, w_vt_all, layer, cos_t, sin_t, seq, alpha):
    t, d = x1.shape
    w = ys.shape[1]
    tm = TM_DISP
    nb = t // tm
    per_b = seq // tm
    n_main = (N_SEG - 1) * SEG
    row_spec = pl.BlockSpec((1, d), lambda i: (0, 0))
    mod_spec = pl.BlockSpec((1, 1, d), lambda i: (i // per_b, 0, 0))
    return pl.pallas_call(
        functools.partial(_combine_inproj_kernel, alpha=alpha),
        grid=(nb,),
        in_specs=[pl.BlockSpec((1, 1, tm), lambda i: (i, 0, 0), memory_space=pltpu.SMEM),
                  pl.BlockSpec((1, 1, tm), lambda i: (jnp.minimum(i + 1, nb - 1), 0, 0),
                               memory_space=pltpu.SMEM),
                  pl.BlockSpec(memory_space=pl.ANY),
                  pl.BlockSpec((tm, d), lambda i: (i, 0)),
                  mod_spec, row_spec, row_spec, mod_spec, mod_spec,
                  pl.BlockSpec((None, d, n_main), lambda i: (layer, 0, 0)),
                  pl.BlockSpec((None, SEG, d), lambda i: (layer, 0, 0)),
                  pl.BlockSpec((tm, LANES), lambda i: (i % per_b, 0)),
                  pl.BlockSpec((tm, LANES), lambda i: (i % per_b, 0))],
        out_specs=[pl.BlockSpec((tm, d), lambda i: (i, 0))]
        + [pl.BlockSpec((tm, SEG), lambda i: (i, 0))] * (N_SEG - 1)
        + [pl.BlockSpec((SEG, tm), lambda i: (0, i))],
        out_shape=[jax.ShapeDtypeStruct((t, d), F32)]
        + [jax.ShapeDtypeStruct((t, SEG), BF16)] * (N_SEG - 1)
        + [jax.ShapeDtypeStruct((SEG, t), BF16)],
        scratch_shapes=[pltpu.VMEM((2, tm, w), U32), pltpu.SemaphoreType.DMA((2,)), _pack_scratch(tm, w)],
        compiler_params=_params("arbitrary"),
        name="combine_inproj",
    )(pos3, pos3, ys, x1, g2, ln_g.reshape(1, d), ln_b.reshape(1, d), sc, sh, w_all, w_vt_all, cos_t, sin_t)


def _class_experts():
    lo, hi = [], []
    for g in range(N_GROUPS):
        for i, j in GROUP_PAIRS:
            lo.append(g * EXPERTS_PER_GROUP + i)
            hi.append(g * EXPERTS_PER_GROUP + j)
    return jnp.array(lo, I32), jnp.array(hi, I32)


def _moe_experts(hp, cls, w_gate, w_up, w_down, layer):
    t = hp.shape[0]
    n_tiles = t // TM_EXP + N_CLASSES
    rank, cnt = _ranks(cls)
    counts = cnt[:N_CLASSES, 0]
    tiles_c = (counts + (TM_EXP - 1)) // TM_EXP
    tile_end = jnp.cumsum(tiles_c)
    end = (tile_end * TM_EXP).astype(I32)
    off = end - (tiles_c * TM_EXP).astype(I32)
    n_valid = tile_end[-1:].astype(I32)
    tile_ids = jnp.arange(n_tiles, dtype=I32)
    tile_cls = jnp.sum(tile_ids[:, None] >= tile_end[None, :], axis=1).astype(I32)
    tile_cls = jnp.minimum(tile_cls, tile_cls[jnp.maximum(n_valid[0] - 1, 0)])
    cls_lo, cls_hi = _class_experts()
    nb = t // TM_DISP
    pos3 = _positions(off, cls, rank).reshape(nb, 1, TM_DISP)
    xs = _dispatch(end, counts, pos3, hp, n_tiles * TM_EXP)
    ys = _experts(cls_lo[tile_cls], cls_hi[tile_cls], n_valid, xs, w_gate, w_up, w_down, layer)
    return pos3, ys


def _rotary_tables(seq):
    half = RET_QK // 2
    inv = 1.0 / (ROPE_BASE ** (jnp.arange(0, RET_QK, 2, dtype=F32) / RET_QK))
    ang = jnp.arange(seq, dtype=F32)[:, None] * inv[None, :]
    cos = jnp.cos(ang)
    sin = jnp.sin(ang)
    reps = LANES // RET_QK
    cos_t = jnp.tile(jnp.concatenate([cos, cos], axis=1), (1, reps))
    sin_t = jnp.tile(jnp.concatenate([-sin, sin], axis=1), (1, reps))
    del half
    return cos_t, sin_t


def kernel(x, c, w_ada, b_ada, w_in, w_out, lambda_q1, lambda_k1, lambda_q2, lambda_k2, diff_subln,
           ln_mix_g, ln_mix_b, ln_ffn_g, ln_ffn_b, router_w, router_b, w_gate, w_up, w_down):
    batch, seq, d = x.shape
    depth = w_ada.shape[0]
    alpha = (2 * depth) ** 0.25
    mod = _adaln(c, w_ada, b_ada)
    cos_t, sin_t = _rotary_tables(seq)
    rw_t = router_w.T.astype(BF16)
    w_in_bf = w_in.astype(BF16)
    w_vt_bf = jnp.swapaxes(w_in_bf[:, :, (N_SEG - 1) * SEG:], 1, 2)
    w_out_bf = w_out.astype(BF16)
    xf = x.reshape(batch * seq, d)
    mods = [[mod[l].reshape(batch, N_MOD, 1, d)[:, i] for i in range(N_MOD)] for l in range(depth)]
    proj = _inproj(xf, mods[0][1], mods[0][0], w_in_bf, w_vt_bf, 0, cos_t, sin_t, seq)
    for l in range(depth):
        _, _, g1, sh2, sc2, g2 = mods[l]
        rq, rk, rv, rg, dq, dk, dvt = proj
        ret = _retention(rq, rk, rv, rg, batch, seq)
        lambda_init = 0.8 - 0.6 * math.exp(-0.3 * l)
        dif = _diff_attention(dq, dk, dvt, lambda_q1[l], lambda_k1[l], lambda_q2[l], lambda_k2[l],
                              diff_subln[l], lambda_init, batch, seq)
        x1, hp, cls = _outproj(ret, dif, xf, g1, sc2, sh2, w_out_bf, l,
                               ln_mix_g[l], ln_mix_b[l], rw_t, router_b, seq, alpha)
        pos3, ys = _moe_experts(hp, cls, w_gate, w_up, w_down, l)
        if l + 1 < depth:
            sh1, sc1 = mods[l + 1][0], mods[l + 1][1]
            xf, *proj = _combine_inproj(pos3, ys, x1, g2, ln_ffn_g[l], ln_ffn_b[l], sc1, sh1,
                                        w_in_bf, w_vt_bf, l + 1, cos_t, sin_t, seq, alpha)
        else:
            xf = _combine(pos3, ys, x1, g2, ln_ffn_g[l], ln_ffn_b[l], seq, alpha)
    return xf.reshape(batch, seq, d)
```

```python
import functools
import math

import jax
import jax.numpy as jnp
from jax import lax
from jax.experimental import pallas as pl
from jax.experimental.pallas import tpu as pltpu

F32 = jnp.float32
BF16 = jnp.bfloat16
U32 = jnp.uint32
I32 = jnp.int32

RET_HEADS = 8
RET_QK = 64
DIFF_HEADS = 4
DIFF_QK = 64
HEAD_PAIR = 128
SEG = 512
N_SEG = 7
CHUNK = 128
ROPE_BASE = 10000.0
SUBLN_EPS = 1e-5
LN_EPS = 1e-5
N_EXPERTS = 16
N_GROUPS = 4
EXPERTS_PER_GROUP = 4
TOP_K = 2
GROUP_PAIRS = tuple((i, j) for i in range(EXPERTS_PER_GROUP) for j in range(i + 1, EXPERTS_PER_GROUP))
N_CLASSES = N_GROUPS * len(GROUP_PAIRS)
CLASS_ROWS = 32
N_MOD = 6
LANES = 128
BF16_SUBLANES = 16
VMEM_LIMIT = 56 * 1024 * 1024

TM_PROJ = 1024
RET_GROUP = 8
CT = 256
BK = 512
TS_RANK = 512
TM_DISP = 512
TM_EXP = 512


def _params(*sem):
    return pltpu.CompilerParams(dimension_semantics=sem, vmem_limit_bytes=VMEM_LIMIT)


def _layer_norm(x):
    mu = jnp.mean(x, axis=-1, keepdims=True)
    xc = x - mu
    var = jnp.mean(xc * xc, axis=-1, keepdims=True)
    return xc * lax.rsqrt(var + LN_EPS)


def _silu(x):
    return x * jax.nn.sigmoid(x)


def _adaln_kernel(c_ref, w_ref, b_ref, o_ref):
    cond = _silu(c_ref[...]).astype(BF16)
    o_ref[0] = jnp.dot(cond, w_ref[0].astype(BF16), preferred_element_type=F32) + b_ref[0]


def _adaln(c, w_ada, b_ada):
    depth, d, n = w_ada.shape
    b = c.shape[0]
    tn = 1536
    return pl.pallas_call(
        _adaln_kernel,
        grid=(depth, n // tn),
        in_specs=[pl.BlockSpec((b, d), lambda l, j: (0, 0)),
                  pl.BlockSpec((1, d, tn), lambda l, j: (l, 0, j)),
                  pl.BlockSpec((1, 1, tn), lambda l, j: (l, 0, j))],
        out_specs=pl.BlockSpec((1, b, tn), lambda l, j: (l, 0, j)),
        out_shape=jax.ShapeDtypeStruct((depth, b, n), F32),
        compiler_params=_params("arbitrary", "arbitrary"),
        name="adaln",
    )(c, w_ada, b_ada.reshape(depth, 1, n))


def _inproj_kernel(x_ref, sc_ref, sh_ref, w_ref, wvt_ref, cos_ref, sin_ref, *out_refs):
    _inproj_body(x_ref[...], sc_ref, sh_ref, w_ref, wvt_ref, cos_ref, sin_ref, out_refs)


def _inproj_body(x, sc_ref, sh_ref, w_ref, wvt_ref, cos_ref, sin_ref, out_refs, after_segment=None):
    h =(_layer_norm(x) * (1.0 + sc_ref[0]) + sh_ref[0]).astype(BF16)
    cos = cos_ref[...]
    sin = sin_ref[...]
    lane = lax.broadcasted_iota(I32, cos.shape, 1)
    first_half = (lane & (RET_QK - 1)) < (RET_QK // 2)

    def rotary(p):
        outs = []
        for c in range(SEG // LANES):
            pc = p[:, c * LANES:(c + 1) * LANES]
            swapped = jnp.where(first_half, pltpu.roll(pc, LANES - RET_QK // 2, 1),
                                pltpu.roll(pc, RET_QK // 2, 1))
            outs.append(pc * cos + swapped * sin)
        return jnp.concatenate(outs, axis=1)

    for s, o_ref in enumerate(out_refs[:-1]):
        p = jnp.dot(h, w_ref[:, s * SEG:(s + 1) * SEG], preferred_element_type=F32)
        if s in (0, 1):
            p = rotary(p)
        if s == 0:
            p = p * (RET_QK ** -0.5)
        if s == 4:
            p = p * (DIFF_QK ** -0.5 * math.log2(math.e))
        o_ref[...] = p.astype(BF16)
        if after_segment is not None:
            after_segment(s)
    out_refs[-1][...] = lax.dot_general(wvt_ref[...], h, (((1,), (1,)), ((), ())),
                                        preferred_element_type=F32).astype(BF16)


def _inproj(x2d, sc, sh, w_all, w_vt_all, layer, cos_t, sin_t, seq):
    t, d = x2d.shape
    tm = TM_PROJ
    per_b = seq // tm
    n_main = (N_SEG - 1) * SEG
    return pl.pallas_call(
        _inproj_kernel,
        grid=(t // tm,),
        in_specs=[pl.BlockSpec((tm, d), lambda i: (i, 0)),
                  pl.BlockSpec((1, 1, d), lambda i: (i // per_b, 0, 0)),
                  pl.BlockSpec((1, 1, d), lambda i: (i // per_b, 0, 0)),
                  pl.BlockSpec((None, d, n_main), lambda i: (layer, 0, 0)),
                  pl.BlockSpec((None, SEG, d), lambda i: (layer, 0, 0)),
                  pl.BlockSpec((tm, LANES), lambda i: (i % per_b, 0)),
                  pl.BlockSpec((tm, LANES), lambda i: (i % per_b, 0))],
        out_specs=[pl.BlockSpec((tm, SEG), lambda i: (i, 0))] * (N_SEG - 1)
        + [pl.BlockSpec((SEG, tm), lambda i: (0, i))],
        out_shape=[jax.ShapeDtypeStruct((t, SEG), BF16)] * (N_SEG - 1)
        + [jax.ShapeDtypeStruct((SEG, t), BF16)],
        compiler_params=_params("arbitrary"),
        name="inproj",
    )(x2d, sc, sh, w_all, w_vt_all, cos_t, sin_t)


def _ret_kernel(q_ref, k_ref, v_ref, g_ref, d2_ref, qd_ref, kd_ref, gm_ref, o_ref, *, n_chunks):
    c = CHUNK
    lo = lax.broadcasted_iota(I32, (c, HEAD_PAIR), 1) < RET_QK
    r = lax.broadcasted_iota(I32, (HEAD_PAIR, HEAD_PAIR), 0) < RET_QK
    cc = lax.broadcasted_iota(I32, (HEAD_PAIR, HEAD_PAIR), 1) < RET_QK
    same_head = r == cc
    d2 = d2_ref[0]
    qd = qd_ref[0]
    kd = kd_ref[0]
    gm = gm_ref[0]

    def split_heads(a):
        zero = jnp.zeros_like(a)
        return jnp.concatenate([jnp.where(lo, a, zero), jnp.where(lo, zero, a)], axis=0)

    grp = RET_GROUP

    def body(t, state):
        sls = [pl.ds(pl.multiple_of((t * grp + g) * c, c), c) for g in range(grp)]
        qs = [q_ref[sl, :] for sl in sls]
        ks = [k_ref[sl, :] for sl in sls]
        vs = [v_ref[sl, :] for sl in sls]
        s2s = [lax.dot_general(split_heads(q), k, (((1,), (1,)), ((), ())), preferred_element_type=F32)
               for q, k in zip(qs, ks)]
        kvs = [lax.dot_general((k.astype(F32) * kd).astype(BF16), v, (((0,), (0,)), ((), ())),
                               preferred_element_type=F32) for k, v in zip(ks, vs)]
        states = []
        for kv in kvs:
            states.append(state)
            state = state * gm + jnp.where(same_head, kv, 0.0)
        outs = []
        for q, v, s2, st in zip(qs, vs, s2s, states):
            p2 = (s2 * d2).astype(BF16)
            pcat = jnp.concatenate([p2[:c], p2[c:]], axis=1)
            inner = jnp.dot(pcat, split_heads(v), preferred_element_type=F32)
            qdq = (q.astype(F32) * qd).astype(BF16)
            outs.append(inner + jnp.dot(qdq, st.astype(BF16), preferred_element_type=F32))
        inv = 1.0 / RET_QK
        for sl, o in zip(sls, outs):
            s_all = jnp.sum(o, axis=-1, keepdims=True)
            s_lo = jnp.sum(jnp.where(lo, o, 0.0), axis=-1, keepdims=True)
            dlt = o - jnp.where(lo, s_lo, s_all - s_lo) * inv
            dd = dlt * dlt
            v_all = jnp.sum(dd, axis=-1, keepdims=True)
            v_lo = jnp.sum(jnp.where(lo, dd, 0.0), axis=-1, keepdims=True)
            var = jnp.where(lo, v_lo, v_all - v_lo) * inv
            y = dlt * lax.rsqrt(var + LN_EPS)
            o_ref[sl, :] = (y * _silu(g_ref[sl, :].astype(F32))).astype(BF16)
        return state

    lax.fori_loop(0, n_chunks // grp, body, jnp.zeros((HEAD_PAIR, HEAD_PAIR), F32))


def _retention_tables():
    c = CHUNK
    log_g = jnp.log(1.0 - 2.0 ** (-5.0 - jnp.arange(RET_HEADS, dtype=F32)))
    idx = jnp.arange(c, dtype=F32)
    rel = idx[:, None] - idx[None, :]
    decay = jnp.where(rel >= 0, jnp.exp(log_g[:, None, None] * jnp.maximum(rel, 0.0)), 0.0)
    d2 = decay.reshape(RET_HEADS // 2, 2 * c, c)
    lane_head = jnp.arange(HEAD_PAIR) // RET_QK
    pair_log = log_g.reshape(RET_HEADS // 2, 2)[:, lane_head]
    qd = jnp.exp(pair_log[:, None, :] * (idx + 1.0)[None, :, None])
    kd = jnp.exp(pair_log[:, None, :] * (c - 1.0 - idx)[None, :, None])
    gm = jnp.broadcast_to(jnp.exp(pair_log * c)[:, :, None], (RET_HEADS // 2, HEAD_PAIR, HEAD_PAIR))
    return d2, qd, kd, gm


def _retention(rq, rk, rv, rg, batch, seq):
    t = rq.shape[0]
    n_pairs = RET_HEADS // 2
    d2, qd, kd, gm = _retention_tables()
    seq_spec = pl.BlockSpec((seq, HEAD_PAIR), lambda b, p: (b, p))
    return pl.pallas_call(
        functools.partial(_ret_kernel, n_chunks=seq // CHUNK),
        grid=(batch, n_pairs),
        in_specs=[seq_spec, seq_spec, seq_spec, seq_spec,
                  pl.BlockSpec((1, 2 * CHUNK, CHUNK), lambda b, p: (p, 0, 0)),
                  pl.BlockSpec((1, CHUNK, HEAD_PAIR), lambda b, p: (p, 0, 0)),
                  pl.BlockSpec((1, CHUNK, HEAD_PAIR), lambda b, p: (p, 0, 0)),
                  pl.BlockSpec((1, HEAD_PAIR, HEAD_PAIR), lambda b, p: (p, 0, 0))],
        out_specs=seq_spec,
        out_shape=jax.ShapeDtypeStruct((t, SEG), BF16),
        compiler_params=_params("arbitrary", "arbitrary"),
        name="retention",
    )(rq, rk, rv, rg, d2, qd, kd, gm)


def _diff_kernel(lq1_ref, lk1_ref, lq2_ref, lk2_ref, sg_ref, q_ref, k_ref, vt_ref, o_ref,
                 vta_ref, sa_ref, sb_ref, mx_ref, m_ref, acc_ref, *, lambda_init):
    bq, bk, ct = q_ref.shape[0], BK, CT
    n_ct = 2 * bq // ct
    s_refs = (sa_ref, sb_ref)

    vta_ref[:HEAD_PAIR, :] = vt_ref[...]
    rid = lax.broadcasted_iota(I32, (vta_ref.shape[0] - HEAD_PAIR, vta_ref.shape[1]), 0)
    vta_ref[HEAD_PAIR:, :] = jnp.where(rid == 0, 1.0, 0.0).astype(BF16)

    lam = (jnp.exp(jnp.sum(lq1_ref[...] * lk1_ref[...], axis=-1, keepdims=True))
           - jnp.exp(jnp.sum(lq2_ref[...] * lk2_ref[...], axis=-1, keepdims=True)) + lambda_init)
    q = q_ref[...]
    lo = lax.broadcasted_iota(I32, q.shape, 1) < DIFF_QK
    zero = jnp.zeros_like(q)
    q2 = jnp.concatenate([jnp.where(lo, q, zero), jnp.where(lo, zero, q)], axis=0)
    q_tiles = [q2[c * ct:(c + 1) * ct, :] for c in range(n_ct)]

    def block_modes(j):
        modes = []
        for c in range(n_ct):
            q_lo = (c * ct) % bq
            if j * bk >= q_lo + ct:
                modes.append("skip")
            elif (j + 1) * bk - 1 <= q_lo:
                modes.append("full")
            else:
                modes.append("masked")
        return tuple(modes)

    def scores(j, slot, modes):
        k_blk = k_ref[j * bk:(j + 1) * bk, :]
        for c, qt in enumerate(q_tiles):
            if modes[c] == "skip":
                continue
            cols = slice(c * ct, (c + 1) * ct)
            s = lax.dot_general(k_blk, qt, (((1,), (1,)), ((), ())), preferred_element_type=F32)
            if modes[c] == "masked":
                kpos = j * bk + lax.broadcasted_iota(I32, s.shape, 0)
                qpos = (c * ct) % bq + lax.broadcasted_iota(I32, s.shape, 1)
                s = jnp.where(kpos <= qpos, s, -jnp.inf)
            s_refs[slot][:, cols] = s
            mx_ref[slot:slot + 1, cols] = jnp.max(s, axis=0, keepdims=True)

    def update(j, slot, modes):
        vta_blk = vta_ref[:, j * bk:(j + 1) * bk]
        for c in range(n_ct):
            if modes[c] == "skip":
                continue
            cols = slice(c * ct, (c + 1) * ct)
            m = m_ref[:, cols]
            m_new = jnp.maximum(m, mx_ref[slot:slot + 1, cols])
            p = jnp.exp2(s_refs[slot][:, cols] - m_new).astype(BF16)
            acc_ref[:, cols] = (jnp.exp2(m - m_new) * acc_ref[:, cols]
                                + jnp.dot(vta_blk, p, preferred_element_type=F32))
            m_ref[:, cols] = m_new

    m_ref[...] = jnp.full(m_ref.shape, -jnp.inf, F32)
    acc_ref[...] = jnp.zeros(acc_ref.shape, F32)
    n_blocks = bq // bk
    scores(0, 0, block_modes(0))
    for j in range(n_blocks):
        if j + 1 < n_blocks:
            scores(j + 1, (j + 1) % 2, block_modes(j + 1))
        update(j, j % 2, block_modes(j))
    acc = acc_ref[...]
    a = acc[:HEAD_PAIR] / acc[HEAD_PAIR:HEAD_PAIR + 1]
    out = (a[:, :bq] - lam * a[:, bq:]).T
    ms = jnp.mean(out * out, axis=-1, keepdims=True)
    out = out * lax.rsqrt(ms + SUBLN_EPS) * sg_ref[...] * (1.0 - lambda_init)
    o_ref[...] = out.astype(BF16)


def _diff_attention(dq, dk, dvt, lq1, lk1, lq2, lk2, subln, lambda_init, batch, seq):
    t = dq.shape[0]
    lam_spec = pl.BlockSpec((1, DIFF_QK), lambda b, h: (0, 0))
    seq_spec = pl.BlockSpec((seq, HEAD_PAIR), lambda b, h: (b, h))
    n_rows = HEAD_PAIR + BF16_SUBLANES
    return pl.pallas_call(
        functools.partial(_diff_kernel, lambda_init=lambda_init),
        grid=(batch, DIFF_HEADS),
        in_specs=[lam_spec, lam_spec, lam_spec, lam_spec,
                  pl.BlockSpec((1, HEAD_PAIR), lambda b, h: (0, 0)),
                  seq_spec, seq_spec,
                  pl.BlockSpec((HEAD_PAIR, seq), lambda b, h: (h, b))],
        out_specs=seq_spec,
        out_shape=jax.ShapeDtypeStruct((t, SEG), BF16),
        scratch_shapes=[pltpu.VMEM((n_rows, seq), BF16),
                        pltpu.VMEM((BK, 2 * seq), F32), pltpu.VMEM((BK, 2 * seq), F32),
                        pltpu.VMEM((2, 2 * seq), F32), pltpu.VMEM((1, 2 * seq), F32),
                        pltpu.VMEM((n_rows, 2 * seq), F32)],
        compiler_params=_params("arbitrary", "arbitrary"),
        name="diff_attention",
    )(lq1.reshape(1, -1), lk1.reshape(1, -1), lq2.reshape(1, -1), lk2.reshape(1, -1),
      subln.reshape(1, -1), dq, dk, dvt)


def _pack_scratch(rows, hw):
    return pltpu.VMEM((hw // LANES, 2 * rows, LANES), F32)


def _packed_zeros(shape):
    rows, hw = shape
    return pltpu.bitcast(jnp.zeros((2 * rows, hw), BF16), U32)


def _store_packed_rows(ref, scr, lo, hi):
    rows, hw = ref.shape
    for c in range(hw // LANES):
        cols = slice(c * LANES, (c + 1) * LANES)
        scr[c, pl.ds(0, rows, stride=2), :] = lo[:, cols]
        scr[c, pl.ds(1, rows, stride=2), :] = hi[:, cols]
    for c in range(hw // LANES):
        ref[:, c * LANES:(c + 1) * LANES] = pltpu.bitcast(scr[c].astype(BF16), U32)


def _load_packed_rows(ref, scr):
    rows, hw = ref.shape
    for c in range(hw // LANES):
        scr[c] = pltpu.bitcast(ref[:, c * LANES:(c + 1) * LANES], BF16).astype(F32)
    lo = jnp.concatenate([scr[c, pl.ds(0, rows, stride=2), :] for c in range(hw // LANES)], axis=1)
    hi = jnp.concatenate([scr[c, pl.ds(1, rows, stride=2), :] for c in range(hw // LANES)], axis=1)
    return lo, hi


def _first_top2(vals):
    m1 = jnp.maximum(jnp.maximum(vals[0], vals[1]), jnp.maximum(vals[2], vals[3]))
    i1 = jnp.where(vals[0] == m1, 0, jnp.where(vals[1] == m1, 1, jnp.where(vals[2] == m1, 2, 3)))
    rest = [jnp.where(i1 == i, -1.0, v) for i, v in enumerate(vals)]
    m2 = jnp.maximum(jnp.maximum(rest[0], rest[1]), jnp.maximum(rest[2], rest[3]))
    i2 = jnp.where(rest[0] == m2, 0, jnp.where(rest[1] == m2, 1, jnp.where(rest[2] == m2, 2, 3)))
    return m1, i1, m2, i2


def _outproj_kernel(ret_ref, dif_ref, x_ref, g1_ref, sc2_ref, sh2_ref, wo_ref, lng_ref, lnb_ref,
                    rw_ref, rb_ref, x1_ref, hp_ref, cls_ref, pack_ref, *, alpha):
    half = ret_ref.shape[1]
    hw = x_ref.shape[1] // 2
    mixed = (jnp.dot(ret_ref[...], wo_ref[:half, :], preferred_element_type=F32)
             + jnp.dot(dif_ref[...], wo_ref[half:, :], preferred_element_type=F32))
    x1 = _layer_norm(alpha * x_ref[...] + g1_ref[0] * mixed) * lng_ref[...] + lnb_ref[...]
    x1_ref[...] = x1
    h2 = _layer_norm(x1) * (1.0 + sc2_ref[0]) + sh2_ref[0]
    logits = lax.dot_general(rw_ref[...], h2.astype(BF16), (((1,), (1,)), ((), ())),
                             preferred_element_type=F32) + rb_ref[...]
    e = jnp.exp(logits - jnp.max(logits, axis=0, keepdims=True))
    probs = e / jnp.sum(e, axis=0, keepdims=True)
    rows = [probs[i:i + 1, :] for i in range(N_EXPERTS)]
    group_tops = []
    for g in range(N_GROUPS):
        m1, _, m2, _ = _first_top2(rows[g * EXPERTS_PER_GROUP:(g + 1) * EXPERTS_PER_GROUP])
        group_tops.append(m1 + m2)
    gmax = jnp.maximum(jnp.maximum(group_tops[0], group_tops[1]), jnp.maximum(group_tops[2], group_tops[3]))
    gbest = jnp.where(group_tops[0] == gmax, 0,
                      jnp.where(group_tops[1] == gmax, 1, jnp.where(group_tops[2] == gmax, 2, 3)))
    sel = []
    for i in range(EXPERTS_PER_GROUP):
        v = rows[3 * EXPERTS_PER_GROUP + i]
        for g in (2, 1, 0):
            v = jnp.where(gbest == g, rows[g * EXPERTS_PER_GROUP + i], v)
        sel.append(v)
    p0, i0, p1, i1 = _first_top2(sel)
    denom = p0 + p1
    first_is_low = i0 < i1
    i_lo = jnp.where(first_is_low, i0, i1)
    i_hi = jnp.where(first_is_low, i1, i0)
    pair_base = jnp.where(i_lo == 0, 0, jnp.where(i_lo == 1, 3, 5))
    cls_ref[...] = gbest * len(GROUP_PAIRS) + pair_base + (i_hi - i_lo - 1)
    g_lo = jnp.where(first_is_low, p0, p1) / denom
    g_hi = jnp.where(first_is_low, p1, p0) / denom
    tm = x1.shape[0]
    rid = lax.broadcasted_iota(I32, (LANES, tm), 0)
    gcols = jnp.where(rid == 0, g_lo, jnp.where(rid == 1, g_hi, 0.0)).T
    _store_packed_rows(hp_ref, pack_ref, jnp.concatenate([h2[:, :hw], gcols], axis=1),
                       jnp.concatenate([h2[:, hw:], gcols], axis=1))


def _outproj(ret, dif, x2d, g1, sc2, sh2, wo_all, layer, ln_g, ln_b, rw_t, rb, seq, alpha):
    t, d = x2d.shape
    tm = TM_PROJ
    per_b = seq // tm
    mod_spec = pl.BlockSpec((1, 1, d), lambda i: (i // per_b, 0, 0))
    row_spec = pl.BlockSpec((1, d), lambda i: (0, 0))
    return pl.pallas_call(
        functools.partial(_outproj_kernel, alpha=alpha),
        grid=(t // tm,),
        in_specs=[pl.BlockSpec((tm, SEG), lambda i: (i, 0)),
                  pl.BlockSpec((tm, SEG), lambda i: (i, 0)),
                  pl.BlockSpec((tm, d), lambda i: (i, 0)),
                  mod_spec, mod_spec, mod_spec,
                  pl.BlockSpec((None, 2 * SEG, d), lambda i: (layer, 0, 0)),
                  row_spec, row_spec,
                  pl.BlockSpec((N_EXPERTS, d), lambda i: (0, 0)),
                  pl.BlockSpec((N_EXPERTS, 1), lambda i: (0, 0))],
        out_specs=[pl.BlockSpec((tm, d), lambda i: (i, 0)),
                   pl.BlockSpec((tm, d // 2 + LANES), lambda i: (i, 0)),
                   pl.BlockSpec((1, tm), lambda i: (0, i))],
        out_shape=[jax.ShapeDtypeStruct((t, d), F32),
                   jax.ShapeDtypeStruct((t, d // 2 + LANES), U32),
                   jax.ShapeDtypeStruct((1, t), I32)],
        scratch_shapes=[_pack_scratch(tm, d // 2 + LANES)],
        compiler_params=_params("arbitrary"),
        name="outproj_router",
    )(ret, dif, x2d, g1, sc2, sh2, wo_all, ln_g.reshape(1, d), ln_b.reshape(1, d), rw_t, rb.reshape(-1, 1))


def _rank_kernel(e_ref, tri_ref, rank_ref, cnt_ref, carry_ref):
    i = pl.program_id(0)

    @pl.when(i == 0)
    def _():
        carry_ref[...] = jnp.zeros_like(carry_ref)

    e = e_ref[...]
    ts = e.shape[1]
    onehot = lax.broadcasted_iota(I32, (CLASS_ROWS, ts), 0) == e
    cum = jnp.dot(jnp.where(onehot, 1.0, 0.0).astype(BF16), tri_ref[...],
                  preferred_element_type=F32)
    carry = carry_ref[...]
    total = cum + carry[:, 0:1]
    rank_ref[...] = (jnp.sum(jnp.where(onehot, total, 0.0), axis=0, keepdims=True) - 1.0).astype(I32)
    new_carry = carry + cum[:, ts - 1:ts]
    carry_ref[...] = new_carry
    cnt_ref[...] = new_carry.astype(I32)


def _ranks(cls):
    n_slots = cls.shape[1]
    ts = TS_RANK
    tri = (jnp.arange(ts)[:, None] <= jnp.arange(ts)[None, :]).astype(BF16)
    return pl.pallas_call(
        _rank_kernel,
        grid=(n_slots // ts,),
        in_specs=[pl.BlockSpec((1, ts), lambda i: (0, i)),
                  pl.BlockSpec((ts, ts), lambda i: (0, 0))],
        out_specs=[pl.BlockSpec((1, ts), lambda i: (0, i)),
                   pl.BlockSpec((CLASS_ROWS, LANES), lambda i: (0, 0))],
        out_shape=[jax.ShapeDtypeStruct((1, n_slots), I32),
                   jax.ShapeDtypeStruct((CLASS_ROWS, LANES), I32)],
        scratch_shapes=[pltpu.VMEM((CLASS_ROWS, LANES), F32)],
        compiler_params=_params("arbitrary"),
        name="slot_ranks",
    )(cls, tri)


def _pos_kernel(off_ref, e_ref, rank_ref, pos_ref):
    e = e_ref[...]
    pos = rank_ref[...]
    for k in range(N_CLASSES):
        pos = pos + jnp.where(e == k, off_ref[k], 0)
    pos_ref[...] = pos


def _positions(off, e_flat, rank):
    n_slots = rank.shape[1]
    ts = min(n_slots, 8192)
    spec = pl.BlockSpec((1, ts), lambda i, *_: (0, i))
    return pl.pallas_call(
        _pos_kernel,
        grid_spec=pltpu.PrefetchScalarGridSpec(
            num_scalar_prefetch=1, grid=(n_slots // ts,), in_specs=[spec, spec], out_specs=spec),
        out_shape=jax.ShapeDtypeStruct((1, n_slots), I32),
        compiler_params=_params("arbitrary"),
        name="slot_positions",
    )(off, e_flat, rank)


def _row_copy(src_ref, r, dst_ref, p, sem):
    return pltpu.make_async_copy(src_ref.at[pl.ds(r, 1)], dst_ref.at[pl.ds(p, 1)], sem)


def _rows_wait(src_ref, dst_ref, sem):
    pltpu.make_async_copy(src_ref, dst_ref, sem).wait()


def _dispatch_kernel(end_ref, cnt_ref, pos_ref, hp_ref, xs_ref, zero_ref, sem, zsem):
    i = pl.program_id(0)
    tm = hp_ref.shape[0]
    tz = zero_ref.shape[0]

    @pl.when(i == 0)
    def _():
        zero_ref[...] = _packed_zeros(zero_ref.shape)
        n_rows = xs_ref.shape[0]
        used = end_ref[N_CLASSES - 1]

        def clear(start):
            return pltpu.make_async_copy(zero_ref, xs_ref.at[pl.ds(pl.multiple_of(start, tz), tz)], zsem)

        for wait in (False, True):
            for e in range(N_CLASSES):
                @pl.when(cnt_ref[e] > 0)
                def _():
                    cp = clear(end_ref[e] - tz)
                    cp.wait() if wait else cp.start()

                @pl.when(used + e * tz < n_rows)
                def _():
                    cp = clear(used + e * tz)
                    cp.wait() if wait else cp.start()

    def issue(h, _):
        for k in range(2):
            r = 2 * h + k
            _row_copy(hp_ref, r, xs_ref, pos_ref[0, 0, r], sem).start(priority=k)
        return 0

    lax.fori_loop(0, tm // 2, issue, 0, unroll=8)
    _rows_wait(hp_ref, xs_ref.at[pl.ds(0, tm)], sem)


def _dispatch(end, cnt, pos3, hp, n_rows):
    t, w = hp.shape
    tm = TM_DISP
    return pl.pallas_call(
        _dispatch_kernel,
        grid_spec=pltpu.PrefetchScalarGridSpec(
            num_scalar_prefetch=2,
            grid=(t // tm,),
            in_specs=[pl.BlockSpec((1, 1, tm), lambda i, *_: (i, 0, 0), memory_space=pltpu.SMEM),
                      pl.BlockSpec((tm, w), lambda i, *_: (i, 0))],
            out_specs=pl.BlockSpec(memory_space=pl.ANY),
            scratch_shapes=[pltpu.VMEM((TM_EXP, w), U32), pltpu.SemaphoreType.DMA(()),
                            pltpu.SemaphoreType.DMA(())]),
        out_shape=jax.ShapeDtypeStruct((n_rows, w), U32),
        compiler_params=_params("arbitrary"),
        name="dispatch",
    )(end, cnt, pos3, hp)


def _expert_kernel(ta_ref, tb_ref, nv_ref, xs_ref, wga_ref, wua_ref, wda_ref, wgb_ref, wub_ref, wdb_ref,
                   ys_ref, unpack_ref, pack_ref):
    i = pl.program_id(0)
    hw = ys_ref.shape[1]

    @pl.when(i < nv_ref[0])
    def _():
        lo, hi = _load_packed_rows(xs_ref, unpack_ref)
        x = jnp.concatenate([lo[:, :hw], hi[:, :hw]], axis=1).astype(BF16)
        gates = lo[:, hw:]

        def mlp(wg_ref, wu_ref, wd_ref):
            g = jnp.dot(x, wg_ref[0].astype(BF16), preferred_element_type=F32)
            u = jnp.dot(x, wu_ref[0].astype(BF16), preferred_element_type=F32)
            he = (_silu(g) * u).astype(BF16)
            return jnp.dot(he, wd_ref[0].astype(BF16), preferred_element_type=F32)

        y = gates[:, 0:1] * mlp(wga_ref, wua_ref, wda_ref) + gates[:, 1:2] * mlp(wgb_ref, wub_ref, wdb_ref)
        _store_packed_rows(ys_ref, pack_ref, y[:, :hw], y[:, hw:])

    @pl.when(i >= nv_ref[0])
    def _():
        ys_ref[...] = _packed_zeros(ys_ref.shape)


def _experts(tile_ea, tile_eb, n_valid, xs, w_gate, w_up, w_down, layer):
    n_rows, w = xs.shape
    tm = TM_EXP
    _, _, d, de = w_gate.shape

    def row_map(i, ta, tb, nv):
        return (jnp.minimum(i, nv[0] - 1), 0)

    def w_spec(shape, first):
        if first:
            return pl.BlockSpec((None,) + shape, lambda i, ta, tb, nv: (layer, ta[i], 0, 0))
        return pl.BlockSpec((None,) + shape, lambda i, ta, tb, nv: (layer, tb[i], 0, 0))

    return pl.pallas_call(
        _expert_kernel,
        grid_spec=pltpu.PrefetchScalarGridSpec(
            num_scalar_prefetch=3,
            grid=(n_rows // tm,),
            in_specs=[pl.BlockSpec((tm, w), row_map),
                      w_spec((1, d, de), True), w_spec((1, d, de), True), w_spec((1, de, d), True),
                      w_spec((1, d, de), False), w_spec((1, d, de), False), w_spec((1, de, d), False)],
            out_specs=pl.BlockSpec((tm, d // 2), lambda i, ta, tb, nv: (i, 0)),
            scratch_shapes=[_pack_scratch(tm, w), _pack_scratch(tm, d // 2)]),
        out_shape=jax.ShapeDtypeStruct((n_rows, d // 2), U32),
        compiler_params=_params("arbitrary"),
        name="experts",
    )(tile_ea, tile_eb, n_valid, xs, w_gate, w_up, w_down, w_gate, w_up, w_down)


def _combine_kernel(pos_ref, pos_next_ref, ys_ref, x1_ref, g2_ref, lng_ref, lnb_ref,
                    o_ref, buf_ref, sems, unpack_ref, *, alpha):
    o_ref[...] = _combine_body(pos_ref, pos_next_ref, ys_ref, x1_ref, g2_ref, lng_ref, lnb_ref,
                               buf_ref, sems, unpack_ref, alpha)


def _combine_inproj_kernel(pos_ref, pos_next_ref, ys_ref, x1_ref, g2_ref, lng_ref, lnb_ref,
                           sc_ref, sh_ref, w_ref, wvt_ref, cos_ref, sin_ref,
                           o_ref, *rest, alpha):
    out_refs, (buf_ref, sems, unpack_ref) = rest[:N_SEG], rest[N_SEG:]
    n_parts = N_SEG - 1
    x2 = _combine_body(pos_ref, pos_next_ref, ys_ref, x1_ref, g2_ref, lng_ref, lnb_ref,
                       buf_ref, sems, unpack_ref, alpha, prefetch_here=False)
    o_ref[...] = x2
    i = pl.program_id(0)
    tm = x1_ref.shape[0]
    bounds = [tm * k // n_parts // 2 for k in range(n_parts + 1)]

    def after_segment(s):
        @pl.when(i + 1 < pl.num_programs(0))
        def _():
            _gather_rows(pos_next_ref, ys_ref, buf_ref.at[1 - lax.rem(i, 2)], sems.at[1 - lax.rem(i, 2)],
                         bounds[s], bounds[s + 1])

    _inproj_body(x2, sc_ref, sh_ref, w_ref, wvt_ref, cos_ref, sin_ref, out_refs, after_segment)


def _gather_rows(p_ref, ys_ref, dst_ref, sem, pair_lo, pair_hi):
    def issue(h, _):
        for k in range(2):
            r = 2 * h + k
            _row_copy(ys_ref, p_ref[0, 0, r], dst_ref, r, sem).start(priority=k)
        return 0

    lax.fori_loop(pair_lo, pair_hi, issue, 0, unroll=8)


def _combine_body(pos_ref, pos_next_ref, ys_ref, x1_ref, g2_ref, lng_ref, lnb_ref,
                  buf_ref, sems, unpack_ref, alpha, prefetch_here=True):
    i = pl.program_id(0)
    tm = x1_ref.shape[0]
    slot = lax.rem(i, 2)

    @pl.when(i == 0)
    def _():
        _gather_rows(pos_ref, ys_ref, buf_ref.at[0], sems.at[0], 0, tm // 2)

    if prefetch_here:
        @pl.when(i + 1 < pl.num_programs(0))
        def _():
            _gather_rows(pos_next_ref, ys_ref, buf_ref.at[1 - slot], sems.at[1 - slot], 0, tm // 2)

    _rows_wait(ys_ref.at[pl.ds(0, tm)], buf_ref.at[slot], sems.at[slot])
    y = jnp.concatenate(_load_packed_rows(buf_ref.at[slot], unpack_ref), axis=1)
    z = alpha * x1_ref[...] + g2_ref[0] * y
    return _layer_norm(z) * lng_ref[...] + lnb_ref[...]


def _combine(pos3, ys, x1, g2, ln_g, ln_b, seq, alpha):
    t, d = x1.shape
    w = ys.shape[1]
    tm = TM_DISP
    nb = t // tm
    per_b = seq // tm
    row_spec = pl.BlockSpec((1, d), lambda i: (0, 0))
    return pl.pallas_call(
        functools.partial(_combine_kernel, alpha=alpha),
        grid=(nb,),
        in_specs=[pl.BlockSpec((1, 1, tm), lambda i: (i, 0, 0), memory_space=pltpu.SMEM),
                  pl.BlockSpec((1, 1, tm), lambda i: (jnp.minimum(i + 1, nb - 1), 0, 0),
                               memory_space=pltpu.SMEM),
                  pl.BlockSpec(memory_space=pl.ANY),
                  pl.BlockSpec((tm, d), lambda i: (i, 0)),
                  pl.BlockSpec((1, 1, d), lambda i: (i // per_b, 0, 0)),
                  row_spec, row_spec],
        out_specs=pl.BlockSpec((tm, d), lambda i: (i, 0)),
        out_shape=jax.ShapeDtypeStruct((t, d), F32),
        scratch_shapes=[pltpu.VMEM((2, tm, w), U32), pltpu.SemaphoreType.DMA((2,)), _pack_scratch(tm, w)],
        compiler_params=_params("arbitrary"),
        name="combine",
    )(pos3, pos3, ys, x1, g2, ln_g.reshape(1, d), ln_b.reshape(1, d))


def _combine_inproj(pos3, ys, x1, g2, ln_g, ln_b, sc, sh, w_all, w_vt_all, layer, cos_t, sin_t, seq, alpha):
    t, d = x1.shape
    w = ys.shape[1]
    tm = TM_DISP
    nb = t // tm
    per_b = seq // tm
    n_main = (N_SEG - 1) * SEG
    row_spec = pl.BlockSpec((1, d), lambda i: (0, 0))
    mod_spec = pl.BlockSpec((1, 1, d), lambda i: (i // per_b, 0, 0))
    return pl.pallas_call(
        functools.partial(_combine_inproj_kernel, alpha=alpha),
        grid=(nb,),
        in_specs=[pl.BlockSpec((1, 1, tm), lambda i: (i, 0, 0), memory_space=pltpu.SMEM),
                  pl.BlockSpec((1, 1, tm), lambda i: (jnp.minimum(i + 1, nb - 1), 0, 0),
                               memory_space=pltpu.SMEM),
                  pl.BlockSpec(memory_space=pl.ANY),
                  pl.BlockSpec((tm, d), lambda i: (i, 0)),
                  mod_spec, row_spec, row_spec, mod_spec, mod_spec,
                  pl.BlockSpec((None, d, n_main), lambda i: (layer, 0, 0)),
                  pl.BlockSpec((None, SEG, d), lambda i: (layer, 0, 0)),
                  pl.BlockSpec((tm, LANES), lambda i: (i % per_b, 0)),
                  pl.BlockSpec((tm, LANES), lambda i: (i % per_b, 0))],
        out_specs=[pl.BlockSpec((tm, d), lambda i: (i, 0))]
        + [pl.BlockSpec((tm, SEG), lambda i: (i, 0))] * (N_SEG - 1)
        + [pl.BlockSpec((SEG, tm), lambda i: (0, i))],
        out_shape=[jax.ShapeDtypeStruct((t, d), F32)]
        + [jax.ShapeDtypeStruct((t, SEG), BF16)] * (N_SEG - 1)
        + [jax.ShapeDtypeStruct((SEG, t), BF16)],
        scratch_shapes=[pltpu.VMEM((2, tm, w), U32), pltpu.SemaphoreType.DMA((2,)), _pack_scratch(tm, w)],
        compiler_params=_params("arbitrary"),
        name="combine_inproj",
    )(pos3, pos3, ys, x1, g2, ln_g.reshape(1, d), ln_b.reshape(1, d), sc, sh, w_all, w_vt_all, cos_t, sin_t)


def _class_experts():
    lo, hi = [], []
    for g in range(N_GROUPS):
        for i, j in GROUP_PAIRS:
            lo.append(g * EXPERTS_PER_GROUP + i)
            hi.append(g * EXPERTS_PER_GROUP + j)
    return jnp.array(lo, I32), jnp.array(hi, I32)


def _moe_experts(hp, cls, w_gate, w_up, w_down, layer):
    t = hp.shape[0]
    n_tiles = t // TM_EXP + N_CLASSES
    rank, cnt = _ranks(cls)
    counts = cnt[:N_CLASSES, 0]
    tiles_c = (counts + (TM_EXP - 1)) // TM_EXP
    tile_end = jnp.cumsum(tiles_c)
    end = (tile_end * TM_EXP).astype(I32)
    off = end - (tiles_c * TM_EXP).astype(I32)
    n_valid = tile_end[-1:].astype(I32)
    tile_ids = jnp.arange(n_tiles, dtype=I32)
    tile_cls = jnp.sum(tile_ids[:, None] >= tile_end[None, :], axis=1).astype(I32)
    tile_cls = jnp.minimum(tile_cls, tile_cls[jnp.maximum(n_valid[0] - 1, 0)])
    cls_lo, cls_hi = _class_experts()
    nb = t // TM_DISP
    pos3 = _positions(off, cls, rank).reshape(nb, 1, TM_DISP)
    xs = _dispatch(end, counts, pos3, hp, n_tiles * TM_EXP)
    ys = _experts(cls_lo[tile_cls], cls_hi[tile_cls], n_valid, xs, w_gate, w_up, w_down, layer)
    return pos3, ys


def _rotary_tables(seq):
    half = RET_QK // 2
    inv = 1.0 / (ROPE_BASE ** (jnp.arange(0, RET_QK, 2, dtype=F32) / RET_QK))
    ang = jnp.arange(seq, dtype=F32)[:, None] * inv[None, :]
    cos = jnp.cos(ang)
    sin = jnp.sin(ang)
    reps = LANES // RET_QK
    cos_t = jnp.tile(jnp.concatenate([cos, cos], axis=1), (1, reps))
    sin_t = jnp.tile(jnp.concatenate([-sin, sin], axis=1), (1, reps))
    del half
    return cos_t, sin_t


def kernel(x, c, w_ada, b_ada, w_in, w_out, lambda_q1, lambda_k1, lambda_q2, lambda_k2, diff_subln,
           ln_mix_g, ln_mix_b, ln_ffn_g, ln_ffn_b, router_w, router_b, w_gate, w_up, w_down):
    batch, seq, d = x.shape
    depth = w_ada.shape[0]
    alpha = (2 * depth) ** 0.25
    mod = _adaln(c, w_ada, b_ada)
    cos_t, sin_t = _rotary_tables(seq)
    rw_t = router_w.T.astype(BF16)
    w_in_bf = w_in.astype(BF16)
    w_vt_bf = jnp.swapaxes(w_in_bf[:, :, (N_SEG - 1) * SEG:], 1, 2)
    w_out_bf = w_out.astype(BF16)
    xf = x.reshape(batch * seq, d)
    mods = [[mod[l].reshape(batch, N_MOD, 1, d)[:, i] for i in range(N_MOD)] for l in range(depth)]
    proj = _inproj(xf, mods[0][1], mods[0][0], w_in_bf, w_vt_bf, 0, cos_t, sin_t, seq)
    for l in range(depth):
        _, _, g1, sh2, sc2, g2 = mods[l]
        rq, rk, rv, rg, dq, dk, dvt = proj
        ret = _retention(rq, rk, rv, rg, batch, seq)
        lambda_init = 0.8 - 0.6 * math.exp(-0.3 * l)
        dif = _diff_attention(dq, dk, dvt, lambda_q1[l], lambda_k1[l], lambda_q2[l], lambda_k2[l],
                              diff_subln[l], lambda_init, batch, seq)
        x1, hp, cls = _outproj(ret, dif, xf, g1, sc2, sh2, w_out_bf, l,
                               ln_mix_g[l], ln_mix_b[l], rw_t, router_b, seq, alpha)
        pos3, ys = _moe_experts(hp, cls, w_gate, w_up, w_down, l)
        if l + 1 < depth:
            sh1, sc1 = mods[l + 1][0], mods[l + 1][1]
            xf, *proj = _combine_inproj(pos3, ys, x1, g2, ln_ffn_g[l], ln_ffn_b[l], sc1, sh1,
                                        w_in_bf, w_vt_bf, l + 1, cos_t, sin_t, seq, alpha)
        else:
            xf = _combine(pos3, ys, x1, g2, ln_ffn_g[l], ln_ffn_b[l], seq, alpha)
    return xf.reshape(batch, seq, d)
```

```python
import functools
import math

import jax
import jax.numpy as jnp
from jax import lax
from jax.experimental import pallas as pl
from jax.experimental.pallas import tpu as pltpu

F32 = jnp.float32
BF16 = jnp.bfloat16
U32 = jnp.uint32
I32 = jnp.int32

RET_HEADS = 8
RET_QK = 64
DIFF_HEADS = 4
DIFF_QK = 64
HEAD_PAIR = 128
SEG = 512
N_SEG = 7
CHUNK = 128
ROPE_BASE = 10000.0
SUBLN_EPS = 1e-5
LN_EPS = 1e-5
N_EXPERTS = 16
N_GROUPS = 4
EXPERTS_PER_GROUP = 4
TOP_K = 2
GROUP_PAIRS = tuple((i, j) for i in range(EXPERTS_PER_GROUP) for j in range(i + 1, EXPERTS_PER_GROUP))
N_CLASSES = N_GROUPS * len(GROUP_PAIRS)
CLASS_ROWS = 32
N_MOD = 6
LANES = 128
BF16_SUBLANES = 16
VMEM_LIMIT = 56 * 1024 * 1024

TM_PROJ = 1024
RET_GROUP = 8
CT = 256
BK = 512
TS_RANK = 512
TM_DISP = 512
TM_EXP = 512


def _params(*sem):
    return pltpu.CompilerParams(dimension_semantics=sem, vmem_limit_bytes=VMEM_LIMIT)


def _layer_norm(x):
    mu = jnp.mean(x, axis=-1, keepdims=True)
    xc = x - mu
    var = jnp.mean(xc * xc, axis=-1, keepdims=True)
    return xc * lax.rsqrt(var + LN_EPS)


def _silu(x):
    return x * jax.nn.sigmoid(x)


def _adaln_kernel(c_ref, w_ref, b_ref, o_ref):
    cond = _silu(c_ref[...]).astype(BF16)
    o_ref[0] = jnp.dot(cond, w_ref[0].astype(BF16), preferred_element_type=F32) + b_ref[0]


def _adaln(c, w_ada, b_ada):
    depth, d, n = w_ada.shape
    b = c.shape[0]
    tn = 1536
    return pl.pallas_call(
        _adaln_kernel,
        grid=(depth, n // tn),
        in_specs=[pl.BlockSpec((b, d), lambda l, j: (0, 0)),
                  pl.BlockSpec((1, d, tn), lambda l, j: (l, 0, j)),
                  pl.BlockSpec((1, 1, tn), lambda l, j: (l, 0, j))],
        out_specs=pl.BlockSpec((1, b, tn), lambda l, j: (l, 0, j)),
        out_shape=jax.ShapeDtypeStruct((depth, b, n), F32),
        compiler_params=_params("arbitrary", "arbitrary"),
        name="adaln",
    )(c, w_ada, b_ada.reshape(depth, 1, n))


def _inproj_kernel(x_ref, sc_ref, sh_ref, w_ref, wvt_ref, cos_ref, sin_ref, *out_refs):
    h = (_layer_norm(x_ref[...]) * (1.0 + sc_ref[0]) + sh_ref[0]).astype(BF16)
    cos = cos_ref[...]
    sin = sin_ref[...]
    lane = lax.broadcasted_iota(I32, cos.shape, 1)
    first_half = (lane & (RET_QK - 1)) < (RET_QK // 2)

    def rotary(p):
        outs = []
        for c in range(SEG // LANES):
            pc = p[:, c * LANES:(c + 1) * LANES]
            swapped = jnp.where(first_half, pltpu.roll(pc, LANES - RET_QK // 2, 1),
                                pltpu.roll(pc, RET_QK // 2, 1))
            outs.append(pc * cos + swapped * sin)
        return jnp.concatenate(outs, axis=1)

    for s, o_ref in enumerate(out_refs[:-1]):
        p = jnp.dot(h, w_ref[:, s * SEG:(s + 1) * SEG], preferred_element_type=F32)
        if s in (0, 1):
            p = rotary(p)
        if s == 0:
            p = p * (RET_QK ** -0.5)
        if s == 4:
            p = p * (DIFF_QK ** -0.5 * math.log2(math.e))
        o_ref[...] = p.astype(BF16)
    out_refs[-1][...] = lax.dot_general(wvt_ref[...], h, (((1,), (1,)), ((), ())),
                                        preferred_element_type=F32).astype(BF16)


def _inproj(x2d, sc, sh, w_all, w_vt_all, layer, cos_t, sin_t, seq):
    t, d = x2d.shape
    tm = TM_PROJ
    per_b = seq // tm
    n_main = (N_SEG - 1) * SEG
    return pl.pallas_call(
        _inproj_kernel,
        grid=(t // tm,),
        in_specs=[pl.BlockSpec((tm, d), lambda i: (i, 0)),
                  pl.BlockSpec((1, 1, d), lambda i: (i // per_b, 0, 0)),
                  pl.BlockSpec((1, 1, d), lambda i: (i // per_b, 0, 0)),
                  pl.BlockSpec((None, d, n_main), lambda i: (layer, 0, 0)),
                  pl.BlockSpec((None, SEG, d), lambda i: (layer, 0, 0)),
                  pl.BlockSpec((tm, LANES), lambda i: (i % per_b, 0)),
                  pl.BlockSpec((tm, LANES), lambda i: (i % per_b, 0))],
        out_specs=[pl.BlockSpec((tm, SEG), lambda i: (i, 0))] * (N_SEG - 1)
        + [pl.BlockSpec((SEG, tm), lambda i: (0, i))],
        out_shape=[jax.ShapeDtypeStruct((t, SEG), BF16)] * (N_SEG - 1)
        + [jax.ShapeDtypeStruct((SEG, t), BF16)],
        compiler_params=_params("arbitrary"),
        name="inproj",
    )(x2d, sc, sh, w_all, w_vt_all, cos_t, sin_t)


def _ret_kernel(q_ref, k_ref, v_ref, g_ref, d2_ref, qd_ref, kd_ref, gm_ref, o_ref, *, n_chunks):
    c = CHUNK
    lo = lax.broadcasted_iota(I32, (c, HEAD_PAIR), 1) < RET_QK
    r = lax.broadcasted_iota(I32, (HEAD_PAIR, HEAD_PAIR), 0) < RET_QK
    cc = lax.broadcasted_iota(I32, (HEAD_PAIR, HEAD_PAIR), 1) < RET_QK
    same_head = r == cc
    d2 = d2_ref[0]
    qd = qd_ref[0]
    kd = kd_ref[0]
    gm = gm_ref[0]

    def split_heads(a):
        zero = jnp.zeros_like(a)
        return jnp.concatenate([jnp.where(lo, a, zero), jnp.where(lo, zero, a)], axis=0)

    grp = RET_GROUP

    def body(t, state):
        sls = [pl.ds(pl.multiple_of((t * grp + g) * c, c), c) for g in range(grp)]
        qs = [q_ref[sl, :] for sl in sls]
        ks = [k_ref[sl, :] for sl in sls]
        vs = [v_ref[sl, :] for sl in sls]
        s2s = [lax.dot_general(split_heads(q), k, (((1,), (1,)), ((), ())), preferred_element_type=F32)
               for q, k in zip(qs, ks)]
        kvs = [lax.dot_general((k.astype(F32) * kd).astype(BF16), v, (((0,), (0,)), ((), ())),
                               preferred_element_type=F32) for k, v in zip(ks, vs)]
        states = []
        for kv in kvs:
            states.append(state)
            state = state * gm + jnp.where(same_head, kv, 0.0)
        outs = []
        for q, v, s2, st in zip(qs, vs, s2s, states):
            p2 = (s2 * d2).astype(BF16)
            pcat = jnp.concatenate([p2[:c], p2[c:]], axis=1)
            inner = jnp.dot(pcat, split_heads(v), preferred_element_type=F32)
            qdq = (q.astype(F32) * qd).astype(BF16)
            outs.append(inner + jnp.dot(qdq, st.astype(BF16), preferred_element_type=F32))
        inv = 1.0 / RET_QK
        for sl, o in zip(sls, outs):
            s_all = jnp.sum(o, axis=-1, keepdims=True)
            s_lo = jnp.sum(jnp.where(lo, o, 0.0), axis=-1, keepdims=True)
            dlt = o - jnp.where(lo, s_lo, s_all - s_lo) * inv
            dd = dlt * dlt
            v_all = jnp.sum(dd, axis=-1, keepdims=True)
            v_lo = jnp.sum(jnp.where(lo, dd, 0.0), axis=-1, keepdims=True)
            var = jnp.where(lo, v_lo, v_all - v_lo) * inv
            y = dlt * lax.rsqrt(var + LN_EPS)
            o_ref[sl, :] = (y * _silu(g_ref[sl, :].astype(F32))).astype(BF16)
        return state

    lax.fori_loop(0, n_chunks // grp, body, jnp.zeros((HEAD_PAIR, HEAD_PAIR), F32))


def _retention_tables():
    c = CHUNK
    log_g = jnp.log(1.0 - 2.0 ** (-5.0 - jnp.arange(RET_HEADS, dtype=F32)))
    idx = jnp.arange(c, dtype=F32)
    rel = idx[:, None] - idx[None, :]
    decay = jnp.where(rel >= 0, jnp.exp(log_g[:, None, None] * jnp.maximum(rel, 0.0)), 0.0)
    d2 = decay.reshape(RET_HEADS // 2, 2 * c, c)
    lane_head = jnp.arange(HEAD_PAIR) // RET_QK
    pair_log = log_g.reshape(RET_HEADS // 2, 2)[:, lane_head]
    qd = jnp.exp(pair_log[:, None, :] * (idx + 1.0)[None, :, None])
    kd = jnp.exp(pair_log[:, None, :] * (c - 1.0 - idx)[None, :, None])
    gm = jnp.broadcast_to(jnp.exp(pair_log * c)[:, :, None], (RET_HEADS // 2, HEAD_PAIR, HEAD_PAIR))
    return d2, qd, kd, gm


def _retention(rq, rk, rv, rg, batch, seq):
    t = rq.shape[0]
    n_pairs = RET_HEADS // 2
    d2, qd, kd, gm = _retention_tables()
    seq_spec = pl.BlockSpec((seq, HEAD_PAIR), lambda b, p: (b, p))
    return pl.pallas_call(
        functools.partial(_ret_kernel, n_chunks=seq // CHUNK),
        grid=(batch, n_pairs),
        in_specs=[seq_spec, seq_spec, seq_spec, seq_spec,
                  pl.BlockSpec((1, 2 * CHUNK, CHUNK), lambda b, p: (p, 0, 0)),
                  pl.BlockSpec((1, CHUNK, HEAD_PAIR), lambda b, p: (p, 0, 0)),
                  pl.BlockSpec((1, CHUNK, HEAD_PAIR), lambda b, p: (p, 0, 0)),
                  pl.BlockSpec((1, HEAD_PAIR, HEAD_PAIR), lambda b, p: (p, 0, 0))],
        out_specs=seq_spec,
        out_shape=jax.ShapeDtypeStruct((t, SEG), BF16),
        compiler_params=_params("arbitrary", "arbitrary"),
        name="retention",
    )(rq, rk, rv, rg, d2, qd, kd, gm)


def _diff_kernel(lq1_ref, lk1_ref, lq2_ref, lk2_ref, sg_ref, q_ref, k_ref, vt_ref, o_ref,
                 vta_ref, sa_ref, sb_ref, mx_ref, m_ref, acc_ref, *, lambda_init):
    bq, bk, ct = q_ref.shape[0], BK, CT
    n_ct = 2 * bq // ct
    s_refs = (sa_ref, sb_ref)

    vta_ref[:HEAD_PAIR, :] = vt_ref[...]
    rid = lax.broadcasted_iota(I32, (vta_ref.shape[0] - HEAD_PAIR, vta_ref.shape[1]), 0)
    vta_ref[HEAD_PAIR:, :] = jnp.where(rid == 0, 1.0, 0.0).astype(BF16)

    lam = (jnp.exp(jnp.sum(lq1_ref[...] * lk1_ref[...], axis=-1, keepdims=True))
           - jnp.exp(jnp.sum(lq2_ref[...] * lk2_ref[...], axis=-1, keepdims=True)) + lambda_init)
    q = q_ref[...]
    lo = lax.broadcasted_iota(I32, q.shape, 1) < DIFF_QK
    zero = jnp.zeros_like(q)
    q2 = jnp.concatenate([jnp.where(lo, q, zero), jnp.where(lo, zero, q)], axis=0)
    q_tiles = [q2[c * ct:(c + 1) * ct, :] for c in range(n_ct)]

    def block_modes(j):
        modes = []
        for c in range(n_ct):
            q_lo = (c * ct) % bq
            if j * bk >= q_lo + ct:
                modes.append("skip")
            elif (j + 1) * bk - 1 <= q_lo:
                modes.append("full")
            else:
                modes.append("masked")
        return tuple(modes)

    def scores(j, slot, modes):
        k_blk = k_ref[j * bk:(j + 1) * bk, :]
        for c, qt in enumerate(q_tiles):
            if modes[c] == "skip":
                continue
            cols = slice(c * ct, (c + 1) * ct)
            s = lax.dot_general(k_blk, qt, (((1,), (1,)), ((), ())), preferred_element_type=F32)
            if modes[c] == "masked":
                kpos = j * bk + lax.broadcasted_iota(I32, s.shape, 0)
                qpos = (c * ct) % bq + lax.broadcasted_iota(I32, s.shape, 1)
                s = jnp.where(kpos <= qpos, s, -jnp.inf)
            s_refs[slot][:, cols] = s
            mx_ref[slot:slot + 1, cols] = jnp.max(s, axis=0, keepdims=True)

    def update(j, slot, modes):
        vta_blk = vta_ref[:, j * bk:(j + 1) * bk]
        for c in range(n_ct):
            if modes[c] == "skip":
                continue
            cols = slice(c * ct, (c + 1) * ct)
            m = m_ref[:, cols]
            m_new = jnp.maximum(m, mx_ref[slot:slot + 1, cols])
            p = jnp.exp2(s_refs[slot][:, cols] - m_new).astype(BF16)
            acc_ref[:, cols] = (jnp.exp2(m - m_new) * acc_ref[:, cols]
                                + jnp.dot(vta_blk, p, preferred_element_type=F32))
            m_ref[:, cols] = m_new

    m_ref[...] = jnp.full(m_ref.shape, -jnp.inf, F32)
    acc_ref[...] = jnp.zeros(acc_ref.shape, F32)
    n_blocks = bq // bk
    scores(0, 0, block_modes(0))
    for j in range(n_blocks):
        if j + 1 < n_blocks:
            scores(j + 1, (j + 1) % 2, block_modes(j + 1))
        update(j, j % 2, block_modes(j))
    acc = acc_ref[...]
    a = acc[:HEAD_PAIR] / acc[HEAD_PAIR:HEAD_PAIR + 1]
    out = (a[:, :bq] - lam * a[:, bq:]).T
    ms = jnp.mean(out * out, axis=-1, keepdims=True)
    out = out * lax.rsqrt(ms + SUBLN_EPS) * sg_ref[...] * (1.0 - lambda_init)
    o_ref[...] = out.astype(BF16)


def _diff_attention(dq, dk, dvt, lq1, lk1, lq2, lk2, subln, lambda_init, batch, seq):
    t = dq.shape[0]
    lam_spec = pl.BlockSpec((1, DIFF_QK), lambda b, h: (0, 0))
    seq_spec = pl.BlockSpec((seq, HEAD_PAIR), lambda b, h: (b, h))
    n_rows = HEAD_PAIR + BF16_SUBLANES
    return pl.pallas_call(
        functools.partial(_diff_kernel, lambda_init=lambda_init),
        grid=(batch, DIFF_HEADS),
        in_specs=[lam_spec, lam_spec, lam_spec, lam_spec,
                  pl.BlockSpec((1, HEAD_PAIR), lambda b, h: (0, 0)),
                  seq_spec, seq_spec,
                  pl.BlockSpec((HEAD_PAIR, seq), lambda b, h: (h, b))],
        out_specs=seq_spec,
        out_shape=jax.ShapeDtypeStruct((t, SEG), BF16),
        scratch_shapes=[pltpu.VMEM((n_rows, seq), BF16),
                        pltpu.VMEM((BK, 2 * seq), F32), pltpu.VMEM((BK, 2 * seq), F32),
                        pltpu.VMEM((2, 2 * seq), F32), pltpu.VMEM((1, 2 * seq), F32),
                        pltpu.VMEM((n_rows, 2 * seq), F32)],
        compiler_params=_params("arbitrary", "arbitrary"),
        name="diff_attention",
    )(lq1.reshape(1, -1), lk1.reshape(1, -1), lq2.reshape(1, -1), lk2.reshape(1, -1),
      subln.reshape(1, -1), dq, dk, dvt)


def _pack_scratch(rows, hw):
    return pltpu.VMEM((hw // LANES, 2 * rows, LANES), F32)


def _packed_zeros(shape):
    rows, hw = shape
    return pltpu.bitcast(jnp.zeros((2 * rows, hw), BF16), U32)


def _store_packed_rows(ref, scr, lo, hi):
    rows, hw = ref.shape
    for c in range(hw // LANES):
        cols = slice(c * LANES, (c + 1) * LANES)
        scr[c, pl.ds(0, rows, stride=2), :] = lo[:, cols]
        scr[c, pl.ds(1, rows, stride=2), :] = hi[:, cols]
    for c in range(hw // LANES):
        ref[:, c * LANES:(c + 1) * LANES] = pltpu.bitcast(scr[c].astype(BF16), U32)


def _load_packed_rows(ref, scr):
    rows, hw = ref.shape
    for c in range(hw // LANES):
        scr[c] = pltpu.bitcast(ref[:, c * LANES:(c + 1) * LANES], BF16).astype(F32)
    lo = jnp.concatenate([scr[c, pl.ds(0, rows, stride=2), :] for c in range(hw // LANES)], axis=1)
    hi = jnp.concatenate([scr[c, pl.ds(1, rows, stride=2), :] for c in range(hw // LANES)], axis=1)
    return lo, hi


def _first_top2(vals):
    m1 = jnp.maximum(jnp.maximum(vals[0], vals[1]), jnp.maximum(vals[2], vals[3]))
    i1 = jnp.where(vals[0] == m1, 0, jnp.where(vals[1] == m1, 1, jnp.where(vals[2] == m1, 2, 3)))
    rest = [jnp.where(i1 == i, -1.0, v) for i, v in enumerate(vals)]
    m2 = jnp.maximum(jnp.maximum(rest[0], rest[1]), jnp.maximum(rest[2], rest[3]))
    i2 = jnp.where(rest[0] == m2, 0, jnp.where(rest[1] == m2, 1, jnp.where(rest[2] == m2, 2, 3)))
    return m1, i1, m2, i2


def _outproj_kernel(ret_ref, dif_ref, x_ref, g1_ref, sc2_ref, sh2_ref, wo_ref, lng_ref, lnb_ref,
                    rw_ref, rb_ref, x1_ref, hp_ref, cls_ref, pack_ref, *, alpha):
    half = ret_ref.shape[1]
    hw = x_ref.shape[1] // 2
    mixed = (jnp.dot(ret_ref[...], wo_ref[:half, :], preferred_element_type=F32)
             + jnp.dot(dif_ref[...], wo_ref[half:, :], preferred_element_type=F32))
    x1 = _layer_norm(alpha * x_ref[...] + g1_ref[0] * mixed) * lng_ref[...] + lnb_ref[...]
    x1_ref[...] = x1
    h2 = _layer_norm(x1) * (1.0 + sc2_ref[0]) + sh2_ref[0]
    logits = lax.dot_general(rw_ref[...], h2.astype(BF16), (((1,), (1,)), ((), ())),
                             preferred_element_type=F32) + rb_ref[...]
    e = jnp.exp(logits - jnp.max(logits, axis=0, keepdims=True))
    probs = e / jnp.sum(e, axis=0, keepdims=True)
    rows = [probs[i:i + 1, :] for i in range(N_EXPERTS)]
    group_tops = []
    for g in range(N_GROUPS):
        m1, _, m2, _ = _first_top2(rows[g * EXPERTS_PER_GROUP:(g + 1) * EXPERTS_PER_GROUP])
        group_tops.append(m1 + m2)
    gmax = jnp.maximum(jnp.maximum(group_tops[0], group_tops[1]), jnp.maximum(group_tops[2], group_tops[3]))
    gbest = jnp.where(group_tops[0] == gmax, 0,
                      jnp.where(group_tops[1] == gmax, 1, jnp.where(group_tops[2] == gmax, 2, 3)))
    sel = []
    for i in range(EXPERTS_PER_GROUP):
        v = rows[3 * EXPERTS_PER_GROUP + i]
        for g in (2, 1, 0):
            v = jnp.where(gbest == g, rows[g * EXPERTS_PER_GROUP + i], v)
        sel.append(v)
    p0, i0, p1, i1 = _first_top2(sel)
    denom = p0 + p1
    first_is_low = i0 < i1
    i_lo = jnp.where(first_is_low, i0, i1)
    i_hi = jnp.where(first_is_low, i1, i0)
    pair_base = jnp.where(i_lo == 0, 0, jnp.where(i_lo == 1, 3, 5))
    cls_ref[...] = gbest * len(GROUP_PAIRS) + pair_base + (i_hi - i_lo - 1)
    g_lo = jnp.where(first_is_low, p0, p1) / denom
    g_hi = jnp.where(first_is_low, p1, p0) / denom
    tm = x1.shape[0]
    rid = lax.broadcasted_iota(I32, (LANES, tm), 0)
    gcols = jnp.where(rid == 0, g_lo, jnp.where(rid == 1, g_hi, 0.0)).T
    _store_packed_rows(hp_ref, pack_ref, jnp.concatenate([h2[:, :hw], gcols], axis=1),
                       jnp.concatenate([h2[:, hw:], gcols], axis=1))


def _outproj(ret, dif, x2d, g1, sc2, sh2, wo_all, layer, ln_g, ln_b, rw_t, rb, seq, alpha):
    t, d = x2d.shape
    tm = TM_PROJ
    per_b = seq // tm
    mod_spec = pl.BlockSpec((1, 1, d), lambda i: (i // per_b, 0, 0))
    row_spec = pl.BlockSpec((1, d), lambda i: (0, 0))
    return pl.pallas_call(
        functools.partial(_outproj_kernel, alpha=alpha),
        grid=(t // tm,),
        in_specs=[pl.BlockSpec((tm, SEG), lambda i: (i, 0)),
                  pl.BlockSpec((tm, SEG), lambda i: (i, 0)),
                  pl.BlockSpec((tm, d), lambda i: (i, 0)),
                  mod_spec, mod_spec, mod_spec,
                  pl.BlockSpec((None, 2 * SEG, d), lambda i: (layer, 0, 0)),
                  row_spec, row_spec,
                  pl.BlockSpec((N_EXPERTS, d), lambda i: (0, 0)),
                  pl.BlockSpec((N_EXPERTS, 1), lambda i: (0, 0))],
        out_specs=[pl.BlockSpec((tm, d), lambda i: (i, 0)),
                   pl.BlockSpec((tm, d // 2 + LANES), lambda i: (i, 0)),
                   pl.BlockSpec((1, tm), lambda i: (0, i))],
        out_shape=[jax.ShapeDtypeStruct((t, d), F32),
                   jax.ShapeDtypeStruct((t, d // 2 + LANES), U32),
                   jax.ShapeDtypeStruct((1, t), I32)],
        scratch_shapes=[_pack_scratch(tm, d // 2 + LANES)],
        compiler_params=_params("arbitrary"),
        name="outproj_router",
    )(ret, dif, x2d, g1, sc2, sh2, wo_all, ln_g.reshape(1, d), ln_b.reshape(1, d), rw_t, rb.reshape(-1, 1))


def _rank_kernel(e_ref, tri_ref, rank_ref, cnt_ref, carry_ref):
    i = pl.program_id(0)

    @pl.when(i == 0)
    def _():
        carry_ref[...] = jnp.zeros_like(carry_ref)

    e = e_ref[...]
    ts = e.shape[1]
    onehot = lax.broadcasted_iota(I32, (CLASS_ROWS, ts), 0) == e
    cum = jnp.dot(jnp.where(onehot, 1.0, 0.0).astype(BF16), tri_ref[...],
                  preferred_element_type=F32)
    carry = carry_ref[...]
    total = cum + carry[:, 0:1]
    rank_ref[...] = (jnp.sum(jnp.where(onehot, total, 0.0), axis=0, keepdims=True) - 1.0).astype(I32)
    new_carry = carry + cum[:, ts - 1:ts]
    carry_ref[...] = new_carry
    cnt_ref[...] = new_carry.astype(I32)


def _ranks(cls):
    n_slots = cls.shape[1]
    ts = TS_RANK
    tri = (jnp.arange(ts)[:, None] <= jnp.arange(ts)[None, :]).astype(BF16)
    return pl.pallas_call(
        _rank_kernel,
        grid=(n_slots // ts,),
        in_specs=[pl.BlockSpec((1, ts), lambda i: (0, i)),
                  pl.BlockSpec((ts, ts), lambda i: (0, 0))],
        out_specs=[pl.BlockSpec((1, ts), lambda i: (0, i)),
                   pl.BlockSpec((CLASS_ROWS, LANES), lambda i: (0, 0))],
        out_shape=[jax.ShapeDtypeStruct((1, n_slots), I32),
                   jax.ShapeDtypeStruct((CLASS_ROWS, LANES), I32)],
        scratch_shapes=[pltpu.VMEM((CLASS_ROWS, LANES), F32)],
        compiler_params=_params("arbitrary"),
        name="slot_ranks",
    )(cls, tri)


def _pos_kernel(off_ref, e_ref, rank_ref, pos_ref):
    e = e_ref[...]
    pos = rank_ref[...]
    for k in range(N_CLASSES):
        pos = pos + jnp.where(e == k, off_ref[k], 0)
    pos_ref[...] = pos


def _positions(off, e_flat, rank):
    n_slots = rank.shape[1]
    ts = min(n_slots, 8192)
    spec = pl.BlockSpec((1, ts), lambda i, *_: (0, i))
    return pl.pallas_call(
        _pos_kernel,
        grid_spec=pltpu.PrefetchScalarGridSpec(
            num_scalar_prefetch=1, grid=(n_slots // ts,), in_specs=[spec, spec], out_specs=spec),
        out_shape=jax.ShapeDtypeStruct((1, n_slots), I32),
        compiler_params=_params("arbitrary"),
        name="slot_positions",
    )(off, e_flat, rank)


def _row_copy(src_ref, r, dst_ref, p, sem):
    return pltpu.make_async_copy(src_ref.at[pl.ds(r, 1)], dst_ref.at[pl.ds(p, 1)], sem)


def _rows_wait(src_ref, dst_ref, sem):
    pltpu.make_async_copy(src_ref, dst_ref, sem).wait()


def _dispatch_kernel(end_ref, cnt_ref, pos_ref, hp_ref, xs_ref, zero_ref, sem, zsem):
    i = pl.program_id(0)
    tm = hp_ref.shape[0]
    tz = zero_ref.shape[0]

    @pl.when(i == 0)
    def _():
        zero_ref[...] = _packed_zeros(zero_ref.shape)
        n_rows = xs_ref.shape[0]
        used = end_ref[N_CLASSES - 1]

        def clear(start):
            return pltpu.make_async_copy(zero_ref, xs_ref.at[pl.ds(pl.multiple_of(start, tz), tz)], zsem)

        for wait in (False, True):
            for e in range(N_CLASSES):
                @pl.when(cnt_ref[e] > 0)
                def _():
                    cp = clear(end_ref[e] - tz)
                    cp.wait() if wait else cp.start()

                @pl.when(used + e * tz < n_rows)
                def _():
                    cp = clear(used + e * tz)
                    cp.wait() if wait else cp.start()

    def issue(h, _):
        for k in range(2):
            r = 2 * h + k
            _row_copy(hp_ref, r, xs_ref, pos_ref[0, 0, r], sem).start(priority=k)
        return 0

    lax.fori_loop(0, tm // 2, issue, 0, unroll=8)
    _rows_wait(hp_ref, xs_ref.at[pl.ds(0, tm)], sem)


def _dispatch(end, cnt, pos3, hp, n_rows):
    t, w = hp.shape
    tm = TM_DISP
    return pl.pallas_call(
        _dispatch_kernel,
        grid_spec=pltpu.PrefetchScalarGridSpec(
            num_scalar_prefetch=2,
            grid=(t // tm,),
            in_specs=[pl.BlockSpec((1, 1, tm), lambda i, *_: (i, 0, 0), memory_space=pltpu.SMEM),
                      pl.BlockSpec((tm, w), lambda i, *_: (i, 0))],
            out_specs=pl.BlockSpec(memory_space=pl.ANY),
            scratch_shapes=[pltpu.VMEM((TM_EXP, w), U32), pltpu.SemaphoreType.DMA(()),
                            pltpu.SemaphoreType.DMA(())]),
        out_shape=jax.ShapeDtypeStruct((n_rows, w), U32),
        compiler_params=_params("arbitrary"),
        name="dispatch",
    )(end, cnt, pos3, hp)


def _expert_kernel(ta_ref, tb_ref, nv_ref, xs_ref, wga_ref, wua_ref, wda_ref, wgb_ref, wub_ref, wdb_ref,
                   ys_ref, unpack_ref, pack_ref):
    i = pl.program_id(0)
    hw = ys_ref.shape[1]

    @pl.when(i < nv_ref[0])
    def _():
        lo, hi = _load_packed_rows(xs_ref, unpack_ref)
        x = jnp.concatenate([lo[:, :hw], hi[:, :hw]], axis=1).astype(BF16)
        gates = lo[:, hw:]

        def mlp(wg_ref, wu_ref, wd_ref):
            g = jnp.dot(x, wg_ref[0].astype(BF16), preferred_element_type=F32)
            u = jnp.dot(x, wu_ref[0].astype(BF16), preferred_element_type=F32)
            he = (_silu(g) * u).astype(BF16)
            return jnp.dot(he, wd_ref[0].astype(BF16), preferred_element_type=F32)

        y = gates[:, 0:1] * mlp(wga_ref, wua_ref, wda_ref) + gates[:, 1:2] * mlp(wgb_ref, wub_ref, wdb_ref)
        _store_packed_rows(ys_ref, pack_ref, y[:, :hw], y[:, hw:])

    @pl.when(i >= nv_ref[0])
    def _():
        ys_ref[...] = _packed_zeros(ys_ref.shape)


def _experts(tile_ea, tile_eb, n_valid, xs, w_gate, w_up, w_down, layer):
    n_rows, w = xs.shape
    tm = TM_EXP
    _, _, d, de = w_gate.shape

    def row_map(i, ta, tb, nv):
        return (jnp.minimum(i, nv[0] - 1), 0)

    def w_spec(shape, first):
        if first:
            return pl.BlockSpec((None,) + shape, lambda i, ta, tb, nv: (layer, ta[i], 0, 0))
        return pl.BlockSpec((None,) + shape, lambda i, ta, tb, nv: (layer, tb[i], 0, 0))

    return pl.pallas_call(
        _expert_kernel,
        grid_spec=pltpu.PrefetchScalarGridSpec(
            num_scalar_prefetch=3,
            grid=(n_rows // tm,),
            in_specs=[pl.BlockSpec((tm, w), row_map),
                      w_spec((1, d, de), True), w_spec((1, d, de), True), w_spec((1, de, d), True),
                      w_spec((1, d, de), False), w_spec((1, d, de), False), w_spec((1, de, d), False)],
            out_specs=pl.BlockSpec((tm, d // 2), lambda i, ta, tb, nv: (i, 0)),
            scratch_shapes=[_pack_scratch(tm, w), _pack_scratch(tm, d // 2)]),
        out_shape=jax.ShapeDtypeStruct((n_rows, d // 2), U32),
        compiler_params=_params("arbitrary"),
        name="experts",
    )(tile_ea, tile_eb, n_valid, xs, w_gate, w_up, w_down, w_gate, w_up, w_down)


def _combine_kernel(pos_ref, pos_next_ref, ys_ref, x1_ref, g2_ref, lng_ref, lnb_ref,
                    o_ref, buf_ref, sems, unpack_ref, *, alpha):
    i = pl.program_id(0)
    tm = x1_ref.shape[0]
    slot = lax.rem(i, 2)

    def gather(p_ref, s):
        def issue(r, _):
            _row_copy(ys_ref, p_ref[0, 0, r], buf_ref.at[s], r, sems.at[s]).start(priority=1)
            return 0

        lax.fori_loop(0, tm, issue, 0, unroll=16)

    @pl.when(i == 0)
    def _():
        gather(pos_ref, 0)

    @pl.when(i + 1 < pl.num_programs(0))
    def _():
        gather(pos_next_ref, 1 - slot)

    _rows_wait(ys_ref.at[pl.ds(0, tm)], buf_ref.at[slot], sems.at[slot])
    y = jnp.concatenate(_load_packed_rows(buf_ref.at[slot], unpack_ref), axis=1)
    z = alpha * x1_ref[...] + g2_ref[0] * y
    o_ref[...] = _layer_norm(z) * lng_ref[...] + lnb_ref[...]


def _combine(pos3, ys, x1, g2, ln_g, ln_b, seq, alpha):
    t, d = x1.shape
    w = ys.shape[1]
    tm = TM_DISP
    nb = t // tm
    per_b = seq // tm
    row_spec = pl.BlockSpec((1, d), lambda i: (0, 0))
    return pl.pallas_call(
        functools.partial(_combine_kernel, alpha=alpha),
        grid=(nb,),
        in_specs=[pl.BlockSpec((1, 1, tm), lambda i: (i, 0, 0), memory_space=pltpu.SMEM),
                  pl.BlockSpec((1, 1, tm), lambda i: (jnp.minimum(i + 1, nb - 1), 0, 0),
                               memory_space=pltpu.SMEM),
                  pl.BlockSpec(memory_space=pl.ANY),
                  pl.BlockSpec((tm, d), lambda i: (i, 0)),
                  pl.BlockSpec((1, 1, d), lambda i: (i // per_b, 0, 0)),
                  row_spec, row_spec],
        out_specs=pl.BlockSpec((tm, d), lambda i: (i, 0)),
        out_shape=jax.ShapeDtypeStruct((t, d), F32),
        scratch_shapes=[pltpu.VMEM((2, tm, w), U32), pltpu.SemaphoreType.DMA((2,)), _pack_scratch(tm, w)],
        compiler_params=_params("arbitrary"),
        name="combine",
    )(pos3, pos3, ys, x1, g2, ln_g.reshape(1, d), ln_b.reshape(1, d))


def _class_experts():
    lo, hi = [], []
    for g in range(N_GROUPS):
        for i, j in GROUP_PAIRS:
            lo.append(g * EXPERTS_PER_GROUP + i)
            hi.append(g * EXPERTS_PER_GROUP + j)
    return jnp.array(lo, I32), jnp.array(hi, I32)


def _moe(hp, cls, x1, g2, ln_g, ln_b, w_gate, w_up, w_down, layer, seq, alpha):
    t = hp.shape[0]
    n_tiles = t // TM_EXP + N_CLASSES
    rank, cnt = _ranks(cls)
    counts = cnt[:N_CLASSES, 0]
    tiles_c = (counts + (TM_EXP - 1)) // TM_EXP
    tile_end = jnp.cumsum(tiles_c)
    end = (tile_end * TM_EXP).astype(I32)
    off = end - (tiles_c * TM_EXP).astype(I32)
    n_valid = tile_end[-1:].astype(I32)
    tile_ids = jnp.arange(n_tiles, dtype=I32)
    tile_cls = jnp.sum(tile_ids[:, None] >= tile_end[None, :], axis=1).astype(I32)
    tile_cls = jnp.minimum(tile_cls, tile_cls[jnp.maximum(n_valid[0] - 1, 0)])
    cls_lo, cls_hi = _class_experts()
    nb = t // TM_DISP
    pos3 = _positions(off, cls, rank).reshape(nb, 1, TM_DISP)
    xs = _dispatch(end, counts, pos3, hp, n_tiles * TM_EXP)
    ys = _experts(cls_lo[tile_cls], cls_hi[tile_cls], n_valid, xs, w_gate, w_up, w_down, layer)
    return _combine(pos3, ys, x1, g2, ln_g, ln_b, seq, alpha)


def _rotary_tables(seq):
    inv = 1.0 / (ROPE_BASE ** (jnp.arange(0, RET_QK, 2, dtype=F32) / RET_QK))
    ang = jnp.arange(seq, dtype=F32)[:, None] * inv[None, :]
    cos = jnp.cos(ang)
    sin = jnp.sin(ang)
    reps = LANES // RET_QK
    cos_t = jnp.tile(jnp.concatenate([cos, cos], axis=1), (1, reps))
    sin_t = jnp.tile(jnp.concatenate([-sin, sin], axis=1), (1, reps))
    return cos_t, sin_t


def kernel(x, c, w_ada, b_ada, w_in, w_out, lambda_q1, lambda_k1, lambda_q2, lambda_k2, diff_subln,
           ln_mix_g, ln_mix_b, ln_ffn_g, ln_ffn_b, router_w, router_b, w_gate, w_up, w_down):
    batch, seq, d = x.shape
    depth = w_ada.shape[0]
    alpha = (2 * depth) ** 0.25
    mod = _adaln(c, w_ada, b_ada)
    cos_t, sin_t = _rotary_tables(seq)
    rw_t = router_w.T.astype(BF16)
    w_in_bf = w_in.astype(BF16)
    w_vt_bf = jnp.swapaxes(w_in[:, :, (N_SEG - 1) * SEG:], 1, 2).astype(BF16)
    w_out_bf = w_out.astype(BF16)
    xf = x.reshape(batch * seq, d)
    for l in range(depth):
        m = mod[l].reshape(batch, N_MOD, 1, d)
        sh1, sc1, g1, sh2, sc2, g2 = (m[:, i] for i in range(N_MOD))
        rq, rk, rv, rg, dq, dk, dvt = _inproj(xf, sc1, sh1, w_in_bf, w_vt_bf, l, cos_t, sin_t, seq)
        ret = _retention(rq, rk, rv, rg, batch, seq)
        lambda_init = 0.8 - 0.6 * math.exp(-0.3 * l)
        dif = _diff_attention(dq, dk, dvt, lambda_q1[l], lambda_k1[l], lambda_q2[l], lambda_k2[l],
                              diff_subln[l], lambda_init, batch, seq)
        x1, hp, cls = _outproj(ret, dif, xf, g1, sc2, sh2, w_out_bf, l,
                               ln_mix_g[l], ln_mix_b[l], rw_t, router_b, seq, alpha)
        xf = _moe(hp, cls, x1, g2, ln_ffn_g[l], ln_ffn_b[l], w_gate, w_up, w_down, l, seq, alpha)
    return xf.reshape(batch, seq, d)
```

```python
import functools
import math

import jax
import jax.numpy as jnp
from jax import lax
from jax.experimental import pallas as pl
from jax.experimental.pallas import tpu as pltpu

F32 = jnp.float32
BF16 = jnp.bfloat16
U32 = jnp.uint32
I32 = jnp.int32

RET_HEADS = 8
RET_QK = 64
DIFF_HEADS = 4
DIFF_QK = 64
HEAD_PAIR = 128
SEG = 512
N_SEG = 7
CHUNK = 128
ROPE_BASE = 10000.0
SUBLN_EPS = 1e-5
LN_EPS = 1e-5
N_EXPERTS = 16
N_GROUPS = 4
EXPERTS_PER_GROUP = 4
TOP_K = 2
GROUP_PAIRS = tuple((i, j) for i in range(EXPERTS_PER_GROUP) for j in range(i + 1, EXPERTS_PER_GROUP))
N_CLASSES = N_GROUPS * len(GROUP_PAIRS)
CLASS_ROWS = 32
N_MOD = 6
LANES = 128
BF16_SUBLANES = 16
VMEM_LIMIT = 56 * 1024 * 1024

TM_PROJ = 1024
RET_GROUP = 8
CT = 256
BK = 512
TM_DISP = 512
TM_EXP = 512


def _params(*sem):
    return pltpu.CompilerParams(dimension_semantics=sem, vmem_limit_bytes=VMEM_LIMIT)


def _layer_norm(x):
    mu = jnp.mean(x, axis=-1, keepdims=True)
    xc = x - mu
    var = jnp.mean(xc * xc, axis=-1, keepdims=True)
    return xc * lax.rsqrt(var + LN_EPS)


def _silu(x):
    return x * jax.nn.sigmoid(x)


def _adaln_kernel(c_ref, w_ref, b_ref, o_ref):
    cond = _silu(c_ref[...]).astype(BF16)
    o_ref[0] = jnp.dot(cond, w_ref[0].astype(BF16), preferred_element_type=F32) + b_ref[0]


def _adaln(c, w_ada, b_ada):
    depth, d, n = w_ada.shape
    b = c.shape[0]
    tn = 1536
    return pl.pallas_call(
        _adaln_kernel,
        grid=(depth, n // tn),
        in_specs=[pl.BlockSpec((b, d), lambda l, j: (0, 0)),
                  pl.BlockSpec((1, d, tn), lambda l, j: (l, 0, j)),
                  pl.BlockSpec((1, 1, tn), lambda l, j: (l, 0, j))],
        out_specs=pl.BlockSpec((1, b, tn), lambda l, j: (l, 0, j)),
        out_shape=jax.ShapeDtypeStruct((depth, b, n), F32),
        compiler_params=_params("arbitrary", "arbitrary"),
        name="adaln",
    )(c, w_ada, b_ada.reshape(depth, 1, n))


def _inproj_kernel(x_ref, sc_ref, sh_ref, w_ref, wvt_ref, cos_ref, sin_ref, *out_refs):
    h = (_layer_norm(x_ref[...]) * (1.0 + sc_ref[0]) + sh_ref[0]).astype(BF16)
    cos = cos_ref[...]
    sin = sin_ref[...]
    lane = lax.broadcasted_iota(I32, cos.shape, 1)
    first_half = (lane & (RET_QK - 1)) < (RET_QK // 2)

    def rotary(p):
        outs = []
        for c in range(SEG // LANES):
            pc = p[:, c * LANES:(c + 1) * LANES]
            swapped = jnp.where(first_half, pltpu.roll(pc, LANES - RET_QK // 2, 1),
                                pltpu.roll(pc, RET_QK // 2, 1))
            outs.append(pc * cos + swapped * sin)
        return jnp.concatenate(outs, axis=1)

    for s, o_ref in enumerate(out_refs[:-1]):
        p = jnp.dot(h, w_ref[:, s * SEG:(s + 1) * SEG], preferred_element_type=F32)
        if s in (0, 1):
            p = rotary(p)
        if s == 0:
            p = p * (RET_QK ** -0.5)
        if s == 4:
            p = p * (DIFF_QK ** -0.5 * math.log2(math.e))
        o_ref[...] = p.astype(BF16)
    out_refs[-1][...] = lax.dot_general(wvt_ref[...], h, (((1,), (1,)), ((), ())),
                                        preferred_element_type=F32).astype(BF16)


def _inproj(x2d, sc, sh, w_all, w_vt_all, layer, cos_t, sin_t, seq):
    t, d = x2d.shape
    tm = TM_PROJ
    per_b = seq // tm
    n_main = (N_SEG - 1) * SEG
    return pl.pallas_call(
        _inproj_kernel,
        grid=(t // tm,),
        in_specs=[pl.BlockSpec((tm, d), lambda i: (i, 0)),
                  pl.BlockSpec((1, 1, d), lambda i: (i // per_b, 0, 0)),
                  pl.BlockSpec((1, 1, d), lambda i: (i // per_b, 0, 0)),
                  pl.BlockSpec((None, d, n_main), lambda i: (layer, 0, 0)),
                  pl.BlockSpec((None, SEG, d), lambda i: (layer, 0, 0)),
                  pl.BlockSpec((tm, LANES), lambda i: (i % per_b, 0)),
                  pl.BlockSpec((tm, LANES), lambda i: (i % per_b, 0))],
        out_specs=[pl.BlockSpec((tm, SEG), lambda i: (i, 0))] * (N_SEG - 1)
        + [pl.BlockSpec((SEG, tm), lambda i: (0, i))],
        out_shape=[jax.ShapeDtypeStruct((t, SEG), BF16)] * (N_SEG - 1)
        + [jax.ShapeDtypeStruct((SEG, t), BF16)],
        compiler_params=_params("arbitrary"),
        name="inproj",
    )(x2d, sc, sh, w_all, w_vt_all, cos_t, sin_t)


def _ret_kernel(q_ref, k_ref, v_ref, g_ref, d2_ref, qd_ref, kd_ref, gm_ref, o_ref, *, n_chunks):
    c = CHUNK
    lo = lax.broadcasted_iota(I32, (c, HEAD_PAIR), 1) < RET_QK
    r = lax.broadcasted_iota(I32, (HEAD_PAIR, HEAD_PAIR), 0) < RET_QK
    cc = lax.broadcasted_iota(I32, (HEAD_PAIR, HEAD_PAIR), 1) < RET_QK
    same_head = r == cc
    d2 = d2_ref[0]
    qd = qd_ref[0]
    kd = kd_ref[0]
    gm = gm_ref[0]

    def split_heads(a):
        zero = jnp.zeros_like(a)
        return jnp.concatenate([jnp.where(lo, a, zero), jnp.where(lo, zero, a)], axis=0)

    grp = RET_GROUP

    def body(t, state):
        sls = [pl.ds(pl.multiple_of((t * grp + g) * c, c), c) for g in range(grp)]
        qs = [q_ref[sl, :] for sl in sls]
        ks = [k_ref[sl, :] for sl in sls]
        vs = [v_ref[sl, :] for sl in sls]
        s2s = [lax.dot_general(split_heads(q), k, (((1,), (1,)), ((), ())), preferred_element_type=F32)
               for q, k in zip(qs, ks)]
        kvs = [lax.dot_general((k.astype(F32) * kd).astype(BF16), v, (((0,), (0,)), ((), ())),
                               preferred_element_type=F32) for k, v in zip(ks, vs)]
        states = []
        for kv in kvs:
            states.append(state)
            state = state * gm + jnp.where(same_head, kv, 0.0)
        outs = []
        for q, v, s2, st in zip(qs, vs, s2s, states):
            p2 = (s2 * d2).astype(BF16)
            pcat = jnp.concatenate([p2[:c], p2[c:]], axis=1)
            inner = jnp.dot(pcat, split_heads(v), preferred_element_type=F32)
            qdq = (q.astype(F32) * qd).astype(BF16)
            outs.append(inner + jnp.dot(qdq, st.astype(BF16), preferred_element_type=F32))
        inv = 1.0 / RET_QK
        for sl, o in zip(sls, outs):
            s_all = jnp.sum(o, axis=-1, keepdims=True)
            s_lo = jnp.sum(jnp.where(lo, o, 0.0), axis=-1, keepdims=True)
            dlt = o - jnp.where(lo, s_lo, s_all - s_lo) * inv
            dd = dlt * dlt
            v_all = jnp.sum(dd, axis=-1, keepdims=True)
            v_lo = jnp.sum(jnp.where(lo, dd, 0.0), axis=-1, keepdims=True)
            var = jnp.where(lo, v_lo, v_all - v_lo) * inv
            y = dlt * lax.rsqrt(var + LN_EPS)
            o_ref[sl, :] = (y * _silu(g_ref[sl, :].astype(F32))).astype(BF16)
        return state

    lax.fori_loop(0, n_chunks // grp, body, jnp.zeros((HEAD_PAIR, HEAD_PAIR), F32))


def _retention_tables():
    c = CHUNK
    log_g = jnp.log(1.0 - 2.0 ** (-5.0 - jnp.arange(RET_HEADS, dtype=F32)))
    idx = jnp.arange(c, dtype=F32)
    rel = idx[:, None] - idx[None, :]
    decay = jnp.where(rel >= 0, jnp.exp(log_g[:, None, None] * jnp.maximum(rel, 0.0)), 0.0)
    d2 = decay.reshape(RET_HEADS // 2, 2 * c, c)
    lane_head = jnp.arange(HEAD_PAIR) // RET_QK
    pair_log = log_g.reshape(RET_HEADS // 2, 2)[:, lane_head]
    qd = jnp.exp(pair_log[:, None, :] * (idx + 1.0)[None, :, None])
    kd = jnp.exp(pair_log[:, None, :] * (c - 1.0 - idx)[None, :, None])
    gm = jnp.broadcast_to(jnp.exp(pair_log * c)[:, :, None], (RET_HEADS // 2, HEAD_PAIR, HEAD_PAIR))
    return d2, qd, kd, gm


def _retention(rq, rk, rv, rg, batch, seq):
    t = rq.shape[0]
    n_pairs = RET_HEADS // 2
    d2, qd, kd, gm = _retention_tables()
    seq_spec = pl.BlockSpec((seq, HEAD_PAIR), lambda b, p: (b, p))
    return pl.pallas_call(
        functools.partial(_ret_kernel, n_chunks=seq // CHUNK),
        grid=(batch, n_pairs),
        in_specs=[seq_spec, seq_spec, seq_spec, seq_spec,
                  pl.BlockSpec((1, 2 * CHUNK, CHUNK), lambda b, p: (p, 0, 0)),
                  pl.BlockSpec((1, CHUNK, HEAD_PAIR), lambda b, p: (p, 0, 0)),
                  pl.BlockSpec((1, CHUNK, HEAD_PAIR), lambda b, p: (p, 0, 0)),
                  pl.BlockSpec((1, HEAD_PAIR, HEAD_PAIR), lambda b, p: (p, 0, 0))],
        out_specs=seq_spec,
        out_shape=jax.ShapeDtypeStruct((t, SEG), BF16),
        compiler_params=_params("arbitrary", "arbitrary"),
        name="retention",
    )(rq, rk, rv, rg, d2, qd, kd, gm)


def _diff_kernel(lq1_ref, lk1_ref, lq2_ref, lk2_ref, sg_ref, q_ref, k_ref, vt_ref, o_ref,
                 vta_ref, sa_ref, sb_ref, mx_ref, m_ref, acc_ref, *, lambda_init):
    bq, bk, ct = q_ref.shape[0], BK, CT
    n_ct = 2 * bq // ct
    s_refs = (sa_ref, sb_ref)

    vta_ref[:HEAD_PAIR, :] = vt_ref[...]
    rid = lax.broadcasted_iota(I32, (vta_ref.shape[0] - HEAD_PAIR, vta_ref.shape[1]), 0)
    vta_ref[HEAD_PAIR:, :] = jnp.where(rid == 0, 1.0, 0.0).astype(BF16)

    lam = (jnp.exp(jnp.sum(lq1_ref[...] * lk1_ref[...], axis=-1, keepdims=True))
           - jnp.exp(jnp.sum(lq2_ref[...] * lk2_ref[...], axis=-1, keepdims=True)) + lambda_init)
    q = q_ref[...]
    lo = lax.broadcasted_iota(I32, q.shape, 1) < DIFF_QK
    zero = jnp.zeros_like(q)
    q2 = jnp.concatenate([jnp.where(lo, q, zero), jnp.where(lo, zero, q)], axis=0)
    q_tiles = [q2[c * ct:(c + 1) * ct, :] for c in range(n_ct)]

    def block_modes(j):
        modes = []
        for c in range(n_ct):
            q_lo = (c * ct) % bq
            if j * bk >= q_lo + ct:
                modes.append("skip")
            elif (j + 1) * bk - 1 <= q_lo:
                modes.append("full")
            else:
                modes.append("masked")
        return tuple(modes)

    def scores(j, slot, modes):
        k_blk = k_ref[j * bk:(j + 1) * bk, :]
        for c, qt in enumerate(q_tiles):
            if modes[c] == "skip":
                continue
            cols = slice(c * ct, (c + 1) * ct)
            s = lax.dot_general(k_blk, qt, (((1,), (1,)), ((), ())), preferred_element_type=F32)
            if modes[c] == "masked":
                kpos = j * bk + lax.broadcasted_iota(I32, s.shape, 0)
                qpos = (c * ct) % bq + lax.broadcasted_iota(I32, s.shape, 1)
                s = jnp.where(kpos <= qpos, s, -jnp.inf)
            s_refs[slot][:, cols] = s
            mx_ref[slot:slot + 1, cols] = jnp.max(s, axis=0, keepdims=True)

    def update(j, slot, modes):
        vta_blk = vta_ref[:, j * bk:(j + 1) * bk]
        for c in range(n_ct):
            if modes[c] == "skip":
                continue
            cols = slice(c * ct, (c + 1) * ct)
            m = m_ref[:, cols]
            m_new = jnp.maximum(m, mx_ref[slot:slot + 1, cols])
            p = jnp.exp2(s_refs[slot][:, cols] - m_new).astype(BF16)
            acc_ref[:, cols] = (jnp.exp2(m - m_new) * acc_ref[:, cols]
                                + jnp.dot(vta_blk, p, preferred_element_type=F32))
            m_ref[:, cols] = m_new

    m_ref[...] = jnp.full(m_ref.shape, -jnp.inf, F32)
    acc_ref[...] = jnp.zeros(acc_ref.shape, F32)
    n_blocks = bq // bk
    scores(0, 0, block_modes(0))
    for j in range(n_blocks):
        if j + 1 < n_blocks:
            scores(j + 1, (j + 1) % 2, block_modes(j + 1))
        update(j, j % 2, block_modes(j))
    acc = acc_ref[...]
    a = acc[:HEAD_PAIR] / acc[HEAD_PAIR:HEAD_PAIR + 1]
    out = (a[:, :bq] - lam * a[:, bq:]).T
    ms = jnp.mean(out * out, axis=-1, keepdims=True)
    out = out * lax.rsqrt(ms + SUBLN_EPS) * sg_ref[...] * (1.0 - lambda_init)
    o_ref[...] = out.astype(BF16)


def _diff_attention(dq, dk, dvt, lq1, lk1, lq2, lk2, subln, lambda_init, batch, seq):
    t = dq.shape[0]
    lam_spec = pl.BlockSpec((1, DIFF_QK), lambda b, h: (0, 0))
    seq_spec = pl.BlockSpec((seq, HEAD_PAIR), lambda b, h: (b, h))
    n_rows = HEAD_PAIR + BF16_SUBLANES
    return pl.pallas_call(
        functools.partial(_diff_kernel, lambda_init=lambda_init),
        grid=(batch, DIFF_HEADS),
        in_specs=[lam_spec, lam_spec, lam_spec, lam_spec,
                  pl.BlockSpec((1, HEAD_PAIR), lambda b, h: (0, 0)),
                  seq_spec, seq_spec,
                  pl.BlockSpec((HEAD_PAIR, seq), lambda b, h: (h, b))],
        out_specs=seq_spec,
        out_shape=jax.ShapeDtypeStruct((t, SEG), BF16),
        scratch_shapes=[pltpu.VMEM((n_rows, seq), BF16),
                        pltpu.VMEM((BK, 2 * seq), F32), pltpu.VMEM((BK, 2 * seq), F32),
                        pltpu.VMEM((2, 2 * seq), F32), pltpu.VMEM((1, 2 * seq), F32),
                        pltpu.VMEM((n_rows, 2 * seq), F32)],
        compiler_params=_params("arbitrary", "arbitrary"),
        name="diff_attention",
    )(lq1.reshape(1, -1), lk1.reshape(1, -1), lq2.reshape(1, -1), lk2.reshape(1, -1),
      subln.reshape(1, -1), dq, dk, dvt)


def _pack_scratch(rows, hw):
    return pltpu.VMEM((hw // LANES, 2 * rows, LANES), F32)


def _packed_zeros(shape):
    rows, hw = shape
    return pltpu.bitcast(jnp.zeros((2 * rows, hw), BF16), U32)


def _store_packed_rows(ref, scr, lo, hi):
    rows, hw = ref.shape
    for c in range(hw // LANES):
        cols = slice(c * LANES, (c + 1) * LANES)
        scr[c, pl.ds(0, rows, stride=2), :] = lo[:, cols]
        scr[c, pl.ds(1, rows, stride=2), :] = hi[:, cols]
    for c in range(hw // LANES):
        ref[:, c * LANES:(c + 1) * LANES] = pltpu.bitcast(scr[c].astype(BF16), U32)


def _load_packed_rows(ref, scr):
    rows, hw = ref.shape
    for c in range(hw // LANES):
        scr[c] = pltpu.bitcast(ref[:, c * LANES:(c + 1) * LANES], BF16).astype(F32)
    lo = jnp.concatenate([scr[c, pl.ds(0, rows, stride=2), :] for c in range(hw // LANES)], axis=1)
    hi = jnp.concatenate([scr[c, pl.ds(1, rows, stride=2), :] for c in range(hw // LANES)], axis=1)
    return lo, hi


def _first_top2(vals):
    m1 = jnp.maximum(jnp.maximum(vals[0], vals[1]), jnp.maximum(vals[2], vals[3]))
    i1 = jnp.where(vals[0] == m1, 0, jnp.where(vals[1] == m1, 1, jnp.where(vals[2] == m1, 2, 3)))
    rest = [jnp.where(i1 == i, -1.0, v) for i, v in enumerate(vals)]
    m2 = jnp.maximum(jnp.maximum(rest[0], rest[1]), jnp.maximum(rest[2], rest[3]))
    i2 = jnp.where(rest[0] == m2, 0, jnp.where(rest[1] == m2, 1, jnp.where(rest[2] == m2, 2, 3)))
    return m1, i1, m2, i2


def _outproj_kernel(ret_ref, dif_ref, x_ref, g1_ref, sc2_ref, sh2_ref, wo_ref, lng_ref, lnb_ref,
                    rw_ref, rb_ref, tri_ref, x1_ref, hp_ref, cls_ref, rank_ref, cnt_ref, pack_ref, carry_ref,
                    *, alpha):
    half = ret_ref.shape[1]
    hw = x_ref.shape[1] // 2
    mixed = (jnp.dot(ret_ref[...], wo_ref[:half, :], preferred_element_type=F32)
             + jnp.dot(dif_ref[...], wo_ref[half:, :], preferred_element_type=F32))
    x1 = _layer_norm(alpha * x_ref[...] + g1_ref[0] * mixed) * lng_ref[...] + lnb_ref[...]
    x1_ref[...] = x1
    h2 = _layer_norm(x1) * (1.0 + sc2_ref[0]) + sh2_ref[0]
    logits = lax.dot_general(rw_ref[...], h2.astype(BF16), (((1,), (1,)), ((), ())),
                             preferred_element_type=F32) + rb_ref[...]
    e = jnp.exp(logits - jnp.max(logits, axis=0, keepdims=True))
    probs = e / jnp.sum(e, axis=0, keepdims=True)
    rows = [probs[i:i + 1, :] for i in range(N_EXPERTS)]
    group_tops = []
    for g in range(N_GROUPS):
        m1, _, m2, _ = _first_top2(rows[g * EXPERTS_PER_GROUP:(g + 1) * EXPERTS_PER_GROUP])
        group_tops.append(m1 + m2)
    gmax = jnp.maximum(jnp.maximum(group_tops[0], group_tops[1]), jnp.maximum(group_tops[2], group_tops[3]))
    gbest = jnp.where(group_tops[0] == gmax, 0,
                      jnp.where(group_tops[1] == gmax, 1, jnp.where(group_tops[2] == gmax, 2, 3)))
    sel = []
    for i in range(EXPERTS_PER_GROUP):
        v = rows[3 * EXPERTS_PER_GROUP + i]
        for g in (2, 1, 0):
            v = jnp.where(gbest == g, rows[g * EXPERTS_PER_GROUP + i], v)
        sel.append(v)
    p0, i0, p1, i1 = _first_top2(sel)
    denom = p0 + p1
    first_is_low = i0 < i1
    i_lo = jnp.where(first_is_low, i0, i1)
    i_hi = jnp.where(first_is_low, i1, i0)
    pair_base = jnp.where(i_lo == 0, 0, jnp.where(i_lo == 1, 3, 5))
    cls = gbest * len(GROUP_PAIRS) + pair_base + (i_hi - i_lo - 1)
    cls_ref[...] = cls
    @pl.when(pl.program_id(0) == 0)
    def _():
        carry_ref[...] = jnp.zeros_like(carry_ref)

    onehot = lax.broadcasted_iota(I32, (CLASS_ROWS, cls.shape[1]), 0) == cls
    cum = jnp.dot(jnp.where(onehot, 1.0, 0.0).astype(BF16), tri_ref[...], preferred_element_type=F32)
    carry = carry_ref[...]
    rank_ref[...] = (jnp.sum(jnp.where(onehot, cum + carry[:, 0:1], 0.0), axis=0, keepdims=True)
                     - 1.0).astype(I32)
    new_carry = carry + cum[:, cls.shape[1] - 1:cls.shape[1]]
    carry_ref[...] = new_carry
    cnt_ref[...] = new_carry.astype(I32)
    g_lo =jnp.where(first_is_low, p0, p1) / denom
    g_hi = jnp.where(first_is_low, p1, p0) / denom
    tm = x1.shape[0]
    rid = lax.broadcasted_iota(I32, (LANES, tm), 0)
    gcols = jnp.where(rid == 0, g_lo, jnp.where(rid == 1, g_hi, 0.0)).T
    _store_packed_rows(hp_ref, pack_ref, jnp.concatenate([h2[:, :hw], gcols], axis=1),
                       jnp.concatenate([h2[:, hw:], gcols], axis=1))


def _outproj(ret, dif, x2d, g1, sc2, sh2, wo_all, layer, ln_g, ln_b, rw_t, rb, seq, alpha):
    t, d = x2d.shape
    tm = TM_PROJ
    per_b = seq // tm
    mod_spec = pl.BlockSpec((1, 1, d), lambda i: (i // per_b, 0, 0))
    row_spec = pl.BlockSpec((1, d), lambda i: (0, 0))
    return pl.pallas_call(
        functools.partial(_outproj_kernel, alpha=alpha),
        grid=(t // tm,),
        in_specs=[pl.BlockSpec((tm, SEG), lambda i: (i, 0)),
                  pl.BlockSpec((tm, SEG), lambda i: (i, 0)),
                  pl.BlockSpec((tm, d), lambda i: (i, 0)),
                  mod_spec, mod_spec, mod_spec,
                  pl.BlockSpec((None, 2 * SEG, d), lambda i: (layer, 0, 0)),
                  row_spec, row_spec,
                  pl.BlockSpec((N_EXPERTS, d), lambda i: (0, 0)),
                  pl.BlockSpec((N_EXPERTS, 1), lambda i: (0, 0)),
                  pl.BlockSpec((tm, tm), lambda i: (0, 0))],
        out_specs=[pl.BlockSpec((tm, d), lambda i: (i, 0)),
                   pl.BlockSpec((tm, d // 2 + LANES), lambda i: (i, 0)),
                   pl.BlockSpec((1, tm), lambda i: (0, i)),
                   pl.BlockSpec((1, tm), lambda i: (0, i)),
                   pl.BlockSpec((CLASS_ROWS, LANES), lambda i: (0, 0))],
        out_shape=[jax.ShapeDtypeStruct((t, d), F32),
                   jax.ShapeDtypeStruct((t, d // 2 + LANES), U32),
                   jax.ShapeDtypeStruct((1, t), I32),
                   jax.ShapeDtypeStruct((1, t), I32),
                   jax.ShapeDtypeStruct((CLASS_ROWS, LANES), I32)],
        scratch_shapes=[_pack_scratch(tm, d // 2 + LANES), pltpu.VMEM((CLASS_ROWS, LANES), F32)],
        compiler_params=_params("arbitrary"),
        name="outproj_router",
    )(ret, dif, x2d, g1, sc2, sh2, wo_all, ln_g.reshape(1, d), ln_b.reshape(1, d), rw_t, rb.reshape(-1, 1),
      (jnp.arange(tm)[:, None] <= jnp.arange(tm)[None, :]).astype(BF16))


def _pos_kernel(off_ref, e_ref, rank_ref, pos_ref):
    e = e_ref[...]
    pos = rank_ref[...]
    for k in range(N_CLASSES):
        pos = pos + jnp.where(e == k, off_ref[k], 0)
    pos_ref[...] = pos


def _positions(off, e_flat, rank):
    n_slots = rank.shape[1]
    ts = min(n_slots, 8192)
    spec = pl.BlockSpec((1, ts), lambda i, *_: (0, i))
    return pl.pallas_call(
        _pos_kernel,
        grid_spec=pltpu.PrefetchScalarGridSpec(
            num_scalar_prefetch=1, grid=(n_slots // ts,), in_specs=[spec, spec], out_specs=spec),
        out_shape=jax.ShapeDtypeStruct((1, n_slots), I32),
        compiler_params=_params("arbitrary"),
        name="slot_positions",
    )(off, e_flat, rank)


def _row_copy(src_ref, r, dst_ref, p, sem):
    return pltpu.make_async_copy(src_ref.at[pl.ds(r, 1)], dst_ref.at[pl.ds(p, 1)], sem)


def _rows_wait(src_ref, dst_ref, sem):
    pltpu.make_async_copy(src_ref, dst_ref, sem).wait()


def _dispatch_kernel(end_ref, cnt_ref, pos_ref, hp_ref, xs_ref, zero_ref, sem, zsem):
    i = pl.program_id(0)
    tm = hp_ref.shape[0]
    tz = zero_ref.shape[0]

    @pl.when(i == 0)
    def _():
        zero_ref[...] = _packed_zeros(zero_ref.shape)
        n_rows = xs_ref.shape[0]
        used = end_ref[N_CLASSES - 1]

        def clear(start):
            return pltpu.make_async_copy(zero_ref, xs_ref.at[pl.ds(pl.multiple_of(start, tz), tz)], zsem)

        for wait in (False, True):
            for e in range(N_CLASSES):
                @pl.when(cnt_ref[e] > 0)
                def _():
                    cp = clear(end_ref[e] - tz)
                    cp.wait() if wait else cp.start()

                @pl.when(used + e * tz < n_rows)
                def _():
                    cp = clear(used + e * tz)
                    cp.wait() if wait else cp.start()

    def issue(h, _):
        for k in range(2):
            r = 2 * h + k
            _row_copy(hp_ref, r, xs_ref, pos_ref[0, 0, r], sem).start(priority=k)
        return 0

    lax.fori_loop(0, tm // 2, issue, 0, unroll=8)
    _rows_wait(hp_ref, xs_ref.at[pl.ds(0, tm)], sem)


def _dispatch(end, cnt, pos3, hp, n_rows):
    t, w = hp.shape
    tm = TM_DISP
    return pl.pallas_call(
        _dispatch_kernel,
        grid_spec=pltpu.PrefetchScalarGridSpec(
            num_scalar_prefetch=2,
            grid=(t // tm,),
            in_specs=[pl.BlockSpec((1, 1, tm), lambda i, *_: (i, 0, 0), memory_space=pltpu.SMEM),
                      pl.BlockSpec((tm, w), lambda i, *_: (i, 0))],
            out_specs=pl.BlockSpec(memory_space=pl.ANY),
            scratch_shapes=[pltpu.VMEM((TM_EXP, w), U32), pltpu.SemaphoreType.DMA(()),
                            pltpu.SemaphoreType.DMA(())]),
        out_shape=jax.ShapeDtypeStruct((n_rows, w), U32),
        compiler_params=_params("arbitrary"),
        name="dispatch",
    )(end, cnt, pos3, hp)


def _expert_kernel(ta_ref, tb_ref, nv_ref, xs_ref, wga_ref, wua_ref, wda_ref, wgb_ref, wub_ref, wdb_ref,
                   ys_ref, unpack_ref, pack_ref):
    i = pl.program_id(0)
    hw = ys_ref.shape[1]

    @pl.when(i < nv_ref[0])
    def _():
        lo, hi = _load_packed_rows(xs_ref, unpack_ref)
        x = jnp.concatenate([lo[:, :hw], hi[:, :hw]], axis=1).astype(BF16)
        gates = lo[:, hw:]

        def mlp(wg_ref, wu_ref, wd_ref):
            g = jnp.dot(x, wg_ref[0].astype(BF16), preferred_element_type=F32)
            u = jnp.dot(x, wu_ref[0].astype(BF16), preferred_element_type=F32)
            he = (_silu(g) * u).astype(BF16)
            return jnp.dot(he, wd_ref[0].astype(BF16), preferred_element_type=F32)

        y = gates[:, 0:1] * mlp(wga_ref, wua_ref, wda_ref) + gates[:, 1:2] * mlp(wgb_ref, wub_ref, wdb_ref)
        _store_packed_rows(ys_ref, pack_ref, y[:, :hw], y[:, hw:])

    @pl.when(i >= nv_ref[0])
    def _():
        ys_ref[...] = _packed_zeros(ys_ref.shape)


def _experts(tile_ea, tile_eb, n_valid, xs, w_gate, w_up, w_down, layer):
    n_rows, w = xs.shape
    tm = TM_EXP
    _, _, d, de = w_gate.shape

    def row_map(i, ta, tb, nv):
        return (jnp.minimum(i, nv[0] - 1), 0)

    def w_spec(shape, first):
        if first:
            return pl.BlockSpec((None,) + shape, lambda i, ta, tb, nv: (layer, ta[i], 0, 0))
        return pl.BlockSpec((None,) + shape, lambda i, ta, tb, nv: (layer, tb[i], 0, 0))

    return pl.pallas_call(
        _expert_kernel,
        grid_spec=pltpu.PrefetchScalarGridSpec(
            num_scalar_prefetch=3,
            grid=(n_rows // tm,),
            in_specs=[pl.BlockSpec((tm, w), row_map),
                      w_spec((1, d, de), True), w_spec((1, d, de), True), w_spec((1, de, d), True),
                      w_spec((1, d, de), False), w_spec((1, d, de), False), w_spec((1, de, d), False)],
            out_specs=pl.BlockSpec((tm, d // 2), lambda i, ta, tb, nv: (i, 0)),
            scratch_shapes=[_pack_scratch(tm, w), _pack_scratch(tm, d // 2)]),
        out_shape=jax.ShapeDtypeStruct((n_rows, d // 2), U32),
        compiler_params=_params("arbitrary"),
        name="experts",
    )(tile_ea, tile_eb, n_valid, xs, w_gate, w_up, w_down, w_gate, w_up, w_down)


def _combine_kernel(pos_ref, pos_next_ref, ys_ref, x1_ref, g2_ref, lng_ref, lnb_ref,
                    o_ref, buf_ref, sems, unpack_ref, *, alpha):
    i = pl.program_id(0)
    tm = x1_ref.shape[0]
    slot = lax.rem(i, 2)

    def gather(p_ref, s):
        def issue(r, _):
            _row_copy(ys_ref, p_ref[0, 0, r], buf_ref.at[s], r, sems.at[s]).start(priority=1)
            return 0

        lax.fori_loop(0, tm, issue, 0, unroll=16)

    @pl.when(i == 0)
    def _():
        gather(pos_ref, 0)

    @pl.when(i + 1 < pl.num_programs(0))
    def _():
        gather(pos_next_ref, 1 - slot)

    _rows_wait(ys_ref.at[pl.ds(0, tm)], buf_ref.at[slot], sems.at[slot])
    y = jnp.concatenate(_load_packed_rows(buf_ref.at[slot], unpack_ref), axis=1)
    z = alpha * x1_ref[...] + g2_ref[0] * y
    o_ref[...] = _layer_norm(z) * lng_ref[...] + lnb_ref[...]


def _combine(pos3, ys, x1, g2, ln_g, ln_b, seq, alpha):
    t, d = x1.shape
    w = ys.shape[1]
    tm = TM_DISP
    nb = t // tm
    per_b = seq // tm
    row_spec = pl.BlockSpec((1, d), lambda i: (0, 0))
    return pl.pallas_call(
        functools.partial(_combine_kernel, alpha=alpha),
        grid=(nb,),
        in_specs=[pl.BlockSpec((1, 1, tm), lambda i: (i, 0, 0), memory_space=pltpu.SMEM),
                  pl.BlockSpec((1, 1, tm), lambda i: (jnp.minimum(i + 1, nb - 1), 0, 0),
                               memory_space=pltpu.SMEM),
                  pl.BlockSpec(memory_space=pl.ANY),
                  pl.BlockSpec((tm, d), lambda i: (i, 0)),
                  pl.BlockSpec((1, 1, d), lambda i: (i // per_b, 0, 0)),
                  row_spec, row_spec],
        out_specs=pl.BlockSpec((tm, d), lambda i: (i, 0)),
        out_shape=jax.ShapeDtypeStruct((t, d), F32),
        scratch_shapes=[pltpu.VMEM((2, tm, w), U32), pltpu.SemaphoreType.DMA((2,)), _pack_scratch(tm, w)],
        compiler_params=_params("arbitrary"),
        name="combine",
    )(pos3, pos3, ys, x1, g2, ln_g.reshape(1, d), ln_b.reshape(1, d))


def _class_experts():
    lo, hi = [], []
    for g in range(N_GROUPS):
        for i, j in GROUP_PAIRS:
            lo.append(g * EXPERTS_PER_GROUP + i)
            hi.append(g * EXPERTS_PER_GROUP + j)
    return jnp.array(lo, I32), jnp.array(hi, I32)


def _moe(hp, cls, rank, cnt, x1, g2, ln_g, ln_b, w_gate, w_up, w_down, layer, seq, alpha):
    t = hp.shape[0]
    n_tiles = t // TM_EXP + N_CLASSES
    counts = cnt[:N_CLASSES, 0]
    tiles_c = (counts + (TM_EXP - 1)) // TM_EXP
    tile_end = jnp.cumsum(tiles_c)
    end = (tile_end * TM_EXP).astype(I32)
    off = end - (tiles_c * TM_EXP).astype(I32)
    n_valid = tile_end[-1:].astype(I32)
    tile_ids = jnp.arange(n_tiles, dtype=I32)
    tile_cls = jnp.sum(tile_ids[:, None] >= tile_end[None, :], axis=1).astype(I32)
    tile_cls = jnp.minimum(tile_cls, tile_cls[jnp.maximum(n_valid[0] - 1, 0)])
    cls_lo, cls_hi = _class_experts()
    nb = t // TM_DISP
    pos3 = _positions(off, cls, rank).reshape(nb, 1, TM_DISP)
    xs = _dispatch(end, counts, pos3, hp, n_tiles * TM_EXP)
    ys = _experts(cls_lo[tile_cls], cls_hi[tile_cls], n_valid, xs, w_gate, w_up, w_down, layer)
    return _combine(pos3, ys, x1, g2, ln_g, ln_b, seq, alpha)


def _rotary_tables(seq):
    inv = 1.0 / (ROPE_BASE ** (jnp.arange(0, RET_QK, 2, dtype=F32) / RET_QK))
    ang = jnp.arange(seq, dtype=F32)[:, None] * inv[None, :]
    cos = jnp.cos(ang)
    sin = jnp.sin(ang)
    reps = LANES // RET_QK
    cos_t = jnp.tile(jnp.concatenate([cos, cos], axis=1), (1, reps))
    sin_t = jnp.tile(jnp.concatenate([-sin, sin], axis=1), (1, reps))
    return cos_t, sin_t


def kernel(x, c, w_ada, b_ada, w_in, w_out, lambda_q1, lambda_k1, lambda_q2, lambda_k2, diff_subln,
           ln_mix_g, ln_mix_b, ln_ffn_g, ln_ffn_b, router_w, router_b, w_gate, w_up, w_down):
    batch, seq, d = x.shape
    depth = w_ada.shape[0]
    alpha = (2 * depth) ** 0.25
    mod = _adaln(c, w_ada, b_ada)
    cos_t, sin_t = _rotary_tables(seq)
    rw_t = router_w.T.astype(BF16)
    w_in_bf = w_in.astype(BF16)
    w_vt_bf = jnp.swapaxes(w_in[:, :, (N_SEG - 1) * SEG:], 1, 2).astype(BF16)
    w_out_bf = w_out.astype(BF16)
    xf = x.reshape(batch * seq, d)
    for l in range(depth):
        m = mod[l].reshape(batch, N_MOD, 1, d)
        sh1, sc1, g1, sh2, sc2, g2 = (m[:, i] for i in range(N_MOD))
        rq, rk, rv, rg, dq, dk, dvt = _inproj(xf, sc1, sh1, w_in_bf, w_vt_bf, l, cos_t, sin_t, seq)
        ret = _retention(rq, rk, rv, rg, batch, seq)
        lambda_init = 0.8 - 0.6 * math.exp(-0.3 * l)
        dif = _diff_attention(dq, dk, dvt, lambda_q1[l], lambda_k1[l], lambda_q2[l], lambda_k2[l],
                              diff_subln[l], lambda_init, batch, seq)
        x1, hp, cls, rank, cnt = _outproj(ret, dif, xf, g1, sc2, sh2, w_out_bf, l,
                                          ln_mix_g[l], ln_mix_b[l], rw_t, router_b, seq, alpha)
        xf = _moe(hp, cls, rank, cnt, x1, g2, ln_ffn_g[l], ln_ffn_b[l], w_gate, w_up, w_down, l, seq, alpha)
    return xf.reshape(batch, seq, d)
```
